```python
import jax, jax.numpy as jnp
from jax import lax
import numpy as np

D_MODEL = 4096
BATCH = 2
SEQ = 4096
DEPTH = 4

D_BRANCH = D_MODEL // 4
D_MIX = 4 * D_BRANCH
NORM_EPS = 1e-6

RET_HEADS = 4
RET_HEAD_DIM = D_BRANCH // RET_HEADS
RET_CHUNK = 256

MOBA_HEADS = 8
MOBA_HEAD_DIM = D_BRANCH // MOBA_HEADS
MOBA_BLOCK = 256
MOBA_TOPK = 3
MOBA_Q_BLOCK = 32

SSM_HEAD_DIM = 64
SSM_HEADS = D_BRANCH // SSM_HEAD_DIM
SSM_GROUPS = 4
SSM_HPG = SSM_HEADS // SSM_GROUPS
SSM_STATE = 128
SSM_CONV = 4
SSM_CHUNK = 256
SSM_CONV_DIM = D_BRANCH + 2 * SSM_GROUPS * SSM_STATE

LRU_BLOCKS = 8
LRU_BLOCK_DIM = D_BRANCH // LRU_BLOCKS
LRU_CONV = 4
LRU_C = 8.0

IN_SIZES = (D_BRANCH, D_BRANCH, D_BRANCH, D_BRANCH,
            D_BRANCH, D_BRANCH, D_BRANCH, D_BRANCH,
            D_BRANCH, SSM_CONV_DIM, SSM_HEADS,
            D_BRANCH, D_BRANCH)
D_IN = 8 * D_BRANCH + D_BRANCH + SSM_CONV_DIM + SSM_HEADS + 2 * D_BRANCH

kernel_name = 'hymba_style_retention_moba_ssd_rglru_trunk'


def rms_norm(x, g):
    xf = x.astype(jnp.float32)
    y = xf * lax.rsqrt(jnp.mean(xf * xf, axis=-1, keepdims=True) + NORM_EPS)
    return (y * g.astype(jnp.float32)).astype(x.dtype)


def pad_seq(t, mult):
    pad = (-t.shape[1]) % mult
    return jnp.pad(t, [(0, 0), (0, pad)] + [(0, 0)] * (t.ndim - 2))


def causal_depthwise_conv(x, w, b):
    width = w.shape[0]
    s = x.shape[1]
    xp = jnp.pad(x, ((0, 0), (width - 1, 0), (0, 0)))
    y = b
    for k in range(width):
        y = y + xp[:, k:k + s] * w[k]
    return y


def retention(q, k, v, gn_w):
    b_, s, _ = q.shape
    H, dh, C = RET_HEADS, RET_HEAD_DIM, RET_CHUNK
    q, k, v = (pad_seq(t.astype(jnp.float32), C) for t in (q, k, v))
    sp = q.shape[1]
    n = sp // C

    def heads(t):
        return t.reshape(b_, n, C, H, dh).transpose(1, 0, 3, 2, 4)

    q, k, v = heads(q), heads(k) * (dh ** -0.5), heads(v)
    log_g = jnp.log1p(-jnp.exp2(-5.0 - jnp.arange(H, dtype=jnp.float32)))
    idx = jnp.arange(C, dtype=jnp.float32)
    diff = idx[:, None] - idx[None, :]
    causal = diff >= 0
    dmat = jnp.where(causal, jnp.exp(log_g[:, None, None] * jnp.where(causal, diff, 0.0)), 0.0)
    q_decay = jnp.exp(log_g[:, None] * (idx + 1.0))[None, :, :, None]
    k_decay = jnp.exp(log_g[:, None] * (C - 1.0 - idx))[None, :, :, None]
    chunk_decay = jnp.exp(log_g * C)[None, :, None, None]

    def step(state, qkv):
        qc, kc, vc = qkv
        inner = jnp.einsum('bhid,bhjd->bhij', qc, kc) * dmat
        y = jnp.einsum('bhij,bhjv->bhiv', inner, vc)
        y = y + jnp.einsum('bhid,bhdv->bhiv', qc, state) * q_decay
        state = chunk_decay * state + jnp.einsum('bhjd,bhjv->bhdv', kc * k_decay, vc)
        return state, y

    state0 = jnp.zeros((b_, H, dh, dh), jnp.float32)
    _, y = lax.scan(step, state0, (q, k, v))
    y = y.transpose(1, 0, 3, 2, 4).reshape(b_, sp, H, dh)[:, :s]
    y = rms_norm(y, gn_w)
    return y.reshape(b_, s, H * dh)


def moba_attention(q, k, v):
    b_, s, _ = q.shape
    H, dh, L, QB = MOBA_HEADS, MOBA_HEAD_DIM, MOBA_BLOCK, MOBA_Q_BLOCK
    scale = dh ** -0.5
    slopes = jnp.exp2(-8.0 * (jnp.arange(H, dtype=jnp.float32) + 1.0) / H)
    qh = q.astype(jnp.float32).reshape(b_, s, H, dh).transpose(0, 2, 1, 3)
    kh = pad_seq(k.astype(jnp.float32), L).reshape(b_, -1, H, dh).transpose(0, 2, 1, 3)
    vh = pad_seq(v.astype(jnp.float32), L).reshape(b_, -1, H, dh).transpose(0, 2, 1, 3)
    nb = kh.shape[2] // L
    ks = min(MOBA_TOPK, nb)
    kb = kh.reshape(b_, H, nb, L, dh)
    vb = vh.reshape(b_, H, nb, L, dh)

    k_mean = kb.mean(axis=3)
    gate = jnp.einsum('bhsd,bhnd->bhsn', qh, k_mean)
    q_blk = jnp.arange(s) // L
    past = jnp.arange(nb)[None, :] < q_blk[:, None]
    gate = jnp.where(past, gate, -jnp.inf)
    _, sel = lax.top_k(gate, ks)
    sel_valid = sel < q_blk[:, None]

    nq = s // QB
    qc = qh.reshape(b_, H, nq, QB, dh).transpose(2, 0, 1, 3, 4)
    selc = sel.reshape(b_, H, nq, QB, ks).transpose(2, 0, 1, 3, 4)
    validc = sel_valid.reshape(b_, H, nq, QB, ks).transpose(2, 0, 1, 3, 4)
    gather_blocks = jax.vmap(jax.vmap(lambda blocks, ids: blocks[ids]))
    offs = jnp.arange(L)

    def one_chunk(args):
        ci, qi, si, vi = args
        qpos = ci * QB + jnp.arange(QB)
        own = (ci * QB) // L
        ksel = gather_blocks(kb, si)
        vsel = gather_blocks(vb, si)
        kpos_sel = si[..., None] * L + offs
        dist_sel = (qpos[:, None, None] - kpos_sel).astype(jnp.float32)
        s_sel = jnp.einsum('bhqd,bhqkld->bhqkl', qi, ksel) * scale - slopes[:, None, None, None] * dist_sel
        s_sel = jnp.where(vi[..., None], s_sel, -jnp.inf)
        kown = lax.dynamic_slice_in_dim(kb, own, 1, axis=2)[:, :, 0]
        vown = lax.dynamic_slice_in_dim(vb, own, 1, axis=2)[:, :, 0]
        kpos_own = own * L + offs
        dist_own = (qpos[:, None] - kpos_own[None, :]).astype(jnp.float32)
        s_own = jnp.einsum('bhqd,bhld->bhql', qi, kown) * scale - slopes[:, None, None] * dist_own
        s_own = jnp.where(kpos_own[None, :] <= qpos[:, None], s_own, -jnp.inf)
        scores = jnp.concatenate([s_sel.reshape(b_, H, QB, ks * L), s_own], axis=-1)
        p = jax.nn.softmax(scores, axis=-1)
        p_sel = p[..., :ks * L].reshape(b_, H, QB, ks, L)
        out = jnp.einsum('bhqkl,bhqkld->bhqd', p_sel, vsel)
        out = out + jnp.einsum('bhql,bhld->bhqd', p[..., ks * L:], vown)
        return out

    o = lax.map(one_chunk, (jnp.arange(nq), qc, selc, validc))
    return o.transpose(1, 0, 3, 2, 4).reshape(b_, s, H * dh)


def mamba2_ssd(z, xbc, dt_raw, conv_w, conv_b, dt_bias, a_log, d_skip, norm_w):
    b_, s, _ = z.shape
    G, R, P, N, Q = SSM_GROUPS, SSM_HPG, SSM_HEAD_DIM, SSM_STATE, SSM_CHUNK
    xbc = jax.nn.silu(causal_depthwise_conv(xbc, conv_w, conv_b)).astype(jnp.float32)
    xs, bm, cm = jnp.split(xbc, [D_BRANCH, D_BRANCH + G * N], axis=-1)
    dt = jax.nn.softplus(dt_raw.astype(jnp.float32) + dt_bias.astype(jnp.float32))
    a = -jnp.exp(a_log.astype(jnp.float32)).reshape(G, R)
    xs = pad_seq(xs.reshape(b_, s, G, R, P), Q)
    bm = pad_seq(bm.reshape(b_, s, G, N), Q)
    cm = pad_seq(cm.reshape(b_, s, G, N), Q)
    dt = pad_seq(dt.reshape(b_, s, G, R), Q)
    sp = xs.shape[1]
    c = sp // Q
    xs = xs.reshape(b_, c, Q, G, R, P)
    bm = bm.reshape(b_, c, Q, G, N)
    cm = cm.reshape(b_, c, Q, G, N)
    dt = dt.reshape(b_, c, Q, G, R)
    da_cs = jnp.cumsum(dt * a, axis=2)

    tri = (jnp.arange(Q)[:, None] >= jnp.arange(Q)[None, :])[None, None, :, :, None, None]
    seg = da_cs[:, :, :, None] - da_cs[:, :, None]
    lmat = jnp.exp(jnp.where(tri, seg, -jnp.inf))
    cb = jnp.einsum('bcign,bcjgn->bcijg', cm, bm)
    wmat = cb[..., None] * lmat * dt[:, :, None]
    y_diag = jnp.einsum('bcijgr,bcjgrp->bcigrp', wmat, xs)

    decay_states = jnp.exp(da_cs[:, :, -1:] - da_cs)
    x_w = xs * (decay_states * dt)[..., None]
    states = jnp.einsum('bclgn,bclgrp->bcgrpn', bm, x_w)
    chunk_decay = jnp.exp(da_cs[:, :, -1])

    def step(h, inp):
        st, dec = inp
        return h * dec[..., None, None] + st, h

    h0 = jnp.zeros((b_, G, R, P, N), jnp.float32)
    _, h_prev = lax.scan(step, h0, (states.swapaxes(0, 1), chunk_decay.swapaxes(0, 1)))
    h_prev = h_prev.swapaxes(0, 1)
    y_off = jnp.einsum('bclgn,bcgrpn->bclgrp', cm, h_prev) * jnp.exp(da_cs)[..., None]
    y = y_diag + y_off + d_skip.astype(jnp.float32).reshape(G, R)[..., None] * xs
    y = y.reshape(b_, sp, D_BRANCH)[:, :s]
    y = y * jax.nn.silu(z.astype(jnp.float32))
    y = rms_norm(y.reshape(b_, s, G, D_BRANCH // G), norm_w.reshape(G, D_BRANCH // G))
    return y.reshape(b_, s, D_BRANCH)


def rg_lru(x, conv_w, conv_b, w_a, b_a, w_x, b_x, lam):
    b_, s, _ = x.shape
    xc = causal_depthwise_conv(x, conv_w, conv_b).astype(jnp.float32)
    xb = xc.reshape(b_, s, LRU_BLOCKS, LRU_BLOCK_DIM)
    r = jax.nn.sigmoid(jnp.einsum('bsnd,nde->bsne', xb, w_a.astype(jnp.float32)).reshape(b_, s, D_BRANCH) + b_a)
    i = jax.nn.sigmoid(jnp.einsum('bsnd,nde->bsne', xb, w_x.astype(jnp.float32)).reshape(b_, s, D_BRANCH) + b_x)
    log_a = -LRU_C * r * jax.nn.softplus(-lam.astype(jnp.float32))
    a = jnp.exp(log_a)
    u = jnp.sqrt(-jnp.expm1(2.0 * log_a)) * (i * xc)

    def combine(left, right):
        a1, b1 = left
        a2, b2 = right
        return a1 * a2, a2 * b1 + b2

    _, h = lax.associative_scan(combine, (a, u), axis=1)
    return h


def hybrid_layer(x, norm_w, w_in, ret_gn_w, ssm_conv_w, ssm_conv_b, ssm_dt_bias, ssm_a_log, ssm_d,
                 ssm_norm_w, lru_conv_w, lru_conv_b, lru_w_a, lru_b_a, lru_w_x, lru_b_x, lru_lambda, w_out):
    h = rms_norm(x, norm_w)
    proj = jnp.einsum('bsd,de->bse', h, w_in)
    (rq, rk, rv, rg, mq, mk, mv, mg, sz, sxbc, sdt, lx, lg) = jnp.split(
        proj, np.cumsum(IN_SIZES)[:-1].tolist(), axis=-1)
    dt_ = h.dtype
    y_ret = (retention(rq, rk, rv, ret_gn_w) * jax.nn.silu(rg.astype(jnp.float32))).astype(dt_)
    y_moba = (moba_attention(mq, mk, mv) * jax.nn.silu(mg.astype(jnp.float32))).astype(dt_)
    y_ssm = mamba2_ssd(sz, sxbc, sdt, ssm_conv_w, ssm_conv_b, ssm_dt_bias, ssm_a_log, ssm_d,
                       ssm_norm_w).astype(dt_)
    y_lru = (rg_lru(lx, lru_conv_w, lru_conv_b, lru_w_a, lru_b_a, lru_w_x, lru_b_x, lru_lambda)
             * jax.nn.silu(lg.astype(jnp.float32))).astype(dt_)
    y = jnp.concatenate([y_ret, y_moba, y_ssm, y_lru], axis=-1)
    return x + jnp.einsum('bse,ed->bsd', y, w_out)


def setup_inputs(seed: int = 0) -> dict:
    key = jax.random.key(seed)
    ks = jax.random.split(key, 24)
    f32 = jnp.float32
    nrm = lambda k, shape, sc: jax.random.normal(k, shape, f32) * sc
    x = nrm(ks[0], (BATCH, SEQ, D_MODEL), 1.0)
    norm_w = 1.0 + nrm(ks[1], (DEPTH, D_MODEL), 0.02)
    w_in = nrm(ks[2], (DEPTH, D_MODEL, D_IN), D_MODEL ** -0.5)
    ret_gn_w = 1.0 + nrm(ks[3], (DEPTH, RET_HEADS, RET_HEAD_DIM), 0.02)
    ssm_conv_w = nrm(ks[4], (DEPTH, SSM_CONV, SSM_CONV_DIM), SSM_CONV ** -0.5)
    ssm_conv_b = nrm(ks[5], (DEPTH, SSM_CONV_DIM), 0.02)
    dt0 = jnp.exp(jax.random.uniform(ks[6], (DEPTH, SSM_HEADS), f32, np.log(1e-3), np.log(1e-1)))
    ssm_dt_bias = dt0 + jnp.log(-jnp.expm1(-dt0))
    ssm_a_log = jnp.log(jax.random.uniform(ks[7], (DEPTH, SSM_HEADS), f32, 1.0, 16.0))
    ssm_d = 1.0 + nrm(ks[8], (DEPTH, SSM_HEADS), 0.1)
    ssm_norm_w = 1.0 + nrm(ks[9], (DEPTH, D_BRANCH), 0.02)
    lru_conv_w = nrm(ks[10], (DEPTH, LRU_CONV, D_BRANCH), LRU_CONV ** -0.5)
    lru_conv_b = nrm(ks[11], (DEPTH, D_BRANCH), 0.02)
    lru_w_a = nrm(ks[12], (DEPTH, LRU_BLOCKS, LRU_BLOCK_DIM, LRU_BLOCK_DIM), LRU_BLOCK_DIM ** -0.5)
    lru_b_a = nrm(ks[13], (DEPTH, D_BRANCH), 0.02)
    lru_w_x = nrm(ks[14], (DEPTH, LRU_BLOCKS, LRU_BLOCK_DIM, LRU_BLOCK_DIM), LRU_BLOCK_DIM ** -0.5)
    lru_b_x = nrm(ks[15], (DEPTH, D_BRANCH), 0.02)
    a_c = jax.random.uniform(ks[16], (DEPTH, D_BRANCH), f32, 0.9, 0.999)
    a_base = a_c ** (1.0 / LRU_C)
    lru_lambda = jnp.log(a_base) - jnp.log1p(-a_base)
    w_out = nrm(ks[17], (DEPTH, D_MIX, D_MODEL), D_MIX ** -0.5)
    final_norm_w = 1.0 + nrm(ks[18], (D_MODEL,), 0.02)
    return {'x': x, 'norm_w': norm_w, 'w_in': w_in, 'ret_gn_w': ret_gn_w,
            'ssm_conv_w': ssm_conv_w, 'ssm_conv_b': ssm_conv_b, 'ssm_dt_bias': ssm_dt_bias,
            'ssm_a_log': ssm_a_log, 'ssm_d': ssm_d, 'ssm_norm_w': ssm_norm_w,
            'lru_conv_w': lru_conv_w, 'lru_conv_b': lru_conv_b, 'lru_w_a': lru_w_a, 'lru_b_a': lru_b_a,
            'lru_w_x': lru_w_x, 'lru_b_x': lru_b_x, 'lru_lambda': lru_lambda,
            'w_out': w_out, 'final_norm_w': final_norm_w}


def reference(x, norm_w, w_in, ret_gn_w, ssm_conv_w, ssm_conv_b, ssm_dt_bias, ssm_a_log, ssm_d,
              ssm_norm_w, lru_conv_w, lru_conv_b, lru_w_a, lru_b_a, lru_w_x, lru_b_x, lru_lambda,
              w_out, final_norm_w):
    for l in range(DEPTH):
        x = hybrid_layer(x, norm_w[l], w_in[l], ret_gn_w[l], ssm_conv_w[l], ssm_conv_b[l],
                         ssm_dt_bias[l], ssm_a_log[l], ssm_d[l], ssm_norm_w[l], lru_conv_w[l],
                         lru_conv_b[l], lru_w_a[l], lru_b_a[l], lru_w_x[l], lru_b_x[l],
                         lru_lambda[l], w_out[l])
    return rms_norm(x, final_norm_w)
```

```python
import functools
import math

import jax
import jax.numpy as jnp
import numpy as np
from jax import lax
from jax.experimental import pallas as pl
from jax.experimental.pallas import tpu as pltpu

F32 = jnp.float32
BF16 = jnp.bfloat16
HIGHEST = lax.Precision.HIGHEST

NORM_EPS = 1e-6
D_BRANCH = 1024
CHUNK = 256
RET_HEADS = 4
RET_HEAD_DIM = 256
MOBA_HEADS = 8
MOBA_HEAD_DIM = 128
MOBA_TOPK = 3
SSM_HEADS = 16
SSM_HEAD_DIM = 64
SSM_GROUPS = 4
SSM_STATE = 128
SSM_CONV_DIM = 2048
CONV_WIDTH = 4
LRU_BLOCKS = 8
LRU_BLOCK_DIM = 128
LRU_C = 8.0
LANES = 128
SUBLANES = 8
VMEM_LIMIT = 56 * 1024 * 1024

COL_RET = 0
COL_MOBA = 4096
COL_XBC = 8192
COL_Z = 10240
COL_LRU_X = 11264
COL_LRU_G = 12288
D_PROJ = 13312

NT_DIMS = (((1,), (1,)), ((), ()))


def _params(*sem):
    return pltpu.CompilerParams(dimension_semantics=sem, vmem_limit_bytes=VMEM_LIMIT)


def _sigmoid(x):
    return 1.0 / (1.0 + jnp.exp(-x))


def _silu(x):
    return x * _sigmoid(x)


def _softplus(x):
    return jnp.maximum(x, 0.0) + jnp.log1p(jnp.exp(-jnp.abs(x)))


def _rmsnorm_kernel(x_ref, w_ref, o_ref):
    x = x_ref[...]
    ms = jnp.mean(x * x, axis=-1, keepdims=True)
    o_ref[...] = (x * lax.rsqrt(ms + NORM_EPS) * w_ref[...]).astype(o_ref.dtype)


def _rmsnorm(x, w, out_dtype):
    t, d = x.shape
    tm = min(512, t)
    return pl.pallas_call(
        _rmsnorm_kernel,
        grid=(t // tm,),
        in_specs=[pl.BlockSpec((tm, d), lambda i: (i, 0)),
                  pl.BlockSpec((1, d), lambda i: (0, 0))],
        out_specs=pl.BlockSpec((tm, d), lambda i: (i, 0)),
        out_shape=jax.ShapeDtypeStruct((t, d), out_dtype),
        compiler_params=_params("parallel"),
        name="rmsnorm",
    )(x, w.reshape(1, d))


def _matmul_kernel(a_ref, w_ref, o_ref):
    o_ref[...] = jnp.dot(a_ref[...], w_ref[...], preferred_element_type=F32).astype(o_ref.dtype)


def _inproj(h, w):
    t, k = h.shape
    n = w.shape[1]
    tm = min(1024, t)
    tn = 1024
    return pl.pallas_call(
        _matmul_kernel,
        grid=(n // tn, t // tm),
        in_specs=[pl.BlockSpec((tm, k), lambda j, i: (i, 0)),
                  pl.BlockSpec((k, tn), lambda j, i: (0, j))],
        out_specs=pl.BlockSpec((tm, tn), lambda j, i: (i, j)),
        out_shape=jax.ShapeDtypeStruct((t, n), F32),
        compiler_params=_params("parallel", "parallel"),
        name="inproj",
    )(h, w)


def _retention_kernel(q_ref, k_ref, v_ref, g_ref, gn_ref, o_ref, state_ref):
    c = pl.program_id(1)

    @pl.when(c == 0)
    def _():
        state_ref[...] = jnp.zeros_like(state_ref)

    C, dh = CHUNK, RET_HEAD_DIM
    row = lax.broadcasted_iota(jnp.int32, (C, C), 0)
    col = lax.broadcasted_iota(jnp.int32, (C, C), 1)
    diff = (row - col).astype(F32)
    rowf = row.astype(F32)
    causal = row >= col
    for h in range(RET_HEADS):
        log_g = math.log1p(-(2.0 ** (-5.0 - h)))
        sl = slice(h * dh, (h + 1) * dh)
        dmat = jnp.where(causal, jnp.exp(log_g * jnp.where(causal, diff, 0.0)), 0.0)
        q_decay = jnp.exp(log_g * (rowf[:, :dh] + 1.0))
        k_decay = jnp.exp(log_g * (C - 1.0 - rowf[:, :dh]))
        q = q_ref[:, sl]
        k = k_ref[:, sl] * (dh ** -0.5)
        v = v_ref[:, sl]
        qb, kb, vb = q.astype(BF16), k.astype(BF16), v.astype(BF16)
        state = state_ref[h]
        inner = lax.dot_general(qb, kb, NT_DIMS, preferred_element_type=F32) * dmat
        y = jnp.dot(inner.astype(BF16), vb, preferred_element_type=F32)
        y = y + jnp.dot(qb, state.astype(BF16), preferred_element_type=F32) * q_decay
        kdT = (k * k_decay).T.astype(BF16)
        state_ref[h] = math.exp(log_g * C) * state + jnp.dot(kdT, vb, preferred_element_type=F32)
        ms = jnp.mean(y * y, axis=-1, keepdims=True)
        y = y * lax.rsqrt(ms + NORM_EPS) * gn_ref[h]
        o_ref[:, sl] = (y * _silu(g_ref[:, sl])).astype(o_ref.dtype)


def _retention(proj, gn_w, batch, seq):
    nc = seq // CHUNK
    t = batch * seq
    base = COL_RET // D_BRANCH

    def spec(j):
        return pl.BlockSpec((CHUNK, D_BRANCH), lambda b, c, j=j: (b * nc + c, base + j))

    return pl.pallas_call(
        _retention_kernel,
        grid=(batch, nc),
        in_specs=[spec(0), spec(1), spec(2), spec(3),
                  pl.BlockSpec((RET_HEADS, 1, RET_HEAD_DIM), lambda b, c: (0, 0, 0))],
        out_specs=pl.BlockSpec((CHUNK, D_BRANCH), lambda b, c: (b * nc + c, 0)),
        out_shape=jax.ShapeDtypeStruct((t, D_BRANCH), BF16),
        scratch_shapes=[pltpu.VMEM((RET_HEADS, RET_HEAD_DIM, RET_HEAD_DIM), F32)],
        compiler_params=_params("parallel", "arbitrary"),
        name="retention",
    )(proj, proj, proj, proj, gn_w.reshape(RET_HEADS, 1, RET_HEAD_DIM))


def _moba_kernel(slopes_ref, q_ref, k_ref, v_ref, g_ref, o_ref, kmean_ref):
    h = pl.program_id(1)
    i = pl.program_id(2)
    L, dh = CHUNK, MOBA_HEAD_DIM
    nb = k_ref.shape[0] // L
    neg = -jnp.inf

    @pl.when(i == 0)
    def _():
        kmean_ref[...] = jnp.zeros_like(kmean_ref)
        for n in range(nb):
            kmean_ref[n:n + 1, :] = jnp.mean(k_ref[n * L:(n + 1) * L, :], axis=0, keepdims=True)

    slope = slopes_ref[h]
    scale = dh ** -0.5
    q = q_ref[...]
    gate = lax.dot_general(q, kmean_ref[...], NT_DIMS, precision=HIGHEST, preferred_element_type=F32)
    lane = lax.broadcasted_iota(jnp.int32, gate.shape, 1)
    lanef = lane.astype(F32)
    gm = jnp.where(lane < i, gate, neg)
    sel = jnp.zeros(gate.shape, F32)
    for _ in range(MOBA_TOPK):
        mx = jnp.max(gm, axis=-1, keepdims=True)
        first = jnp.min(jnp.where(gm == mx, lanef, float(LANES)), axis=-1, keepdims=True)
        pick = jnp.logical_and(lanef == first, mx > neg)
        sel = jnp.where(pick, 1.0, sel)
        gm = jnp.where(pick, neg, gm)

    qb = q.astype(BF16)
    row = lax.broadcasted_iota(jnp.int32, (L, L), 0)
    col = lax.broadcasted_iota(jnp.int32, (L, L), 1)
    bias0 = -slope * (row - col).astype(F32)

    own = pl.multiple_of(i * L, L)
    k_own = k_ref[pl.ds(own, L), :].astype(BF16)
    v_own = v_ref[pl.ds(own, L), :].astype(BF16)
    s = lax.dot_general(qb, k_own, NT_DIMS, preferred_element_type=F32) * scale + bias0
    s = jnp.where(col <= row, s, neg)
    m0 = jnp.max(s, axis=-1, keepdims=True)
    p = jnp.exp(s - m0)
    l0 = jnp.sum(p, axis=-1, keepdims=True)
    acc0 = jnp.dot(p.astype(BF16), v_own, preferred_element_type=F32)

    def body(j, carry):
        m, l, acc = carry
        off = pl.multiple_of(j * L, L)
        kj = k_ref[pl.ds(off, L), :].astype(BF16)
        vj = v_ref[pl.ds(off, L), :].astype(BF16)
        selj = jnp.max(jnp.where(lane == j, sel, 0.0), axis=-1, keepdims=True)
        blk_bias = jnp.where(selj > 0.0, -slope * ((i - j) * L).astype(F32), neg)
        sj = lax.dot_general(qb, kj, NT_DIMS, preferred_element_type=F32) * scale + bias0 + blk_bias
        m_new = jnp.maximum(m, jnp.max(sj, axis=-1, keepdims=True))
        alpha = jnp.exp(m - m_new)
        pj = jnp.exp(sj - m_new)
        l = l * alpha + jnp.sum(pj, axis=-1, keepdims=True)
        acc = acc * alpha + jnp.dot(pj.astype(BF16), vj, preferred_element_type=F32)
        return m_new, l, acc

    _, l, acc = lax.fori_loop(0, i, body, (m0, l0, acc0))
    o_ref[...] = (acc / l * _silu(g_ref[...])).astype(o_ref.dtype)


def _moba(proj, batch, seq):
    nb = seq // CHUNK
    t = batch * seq
    base = COL_MOBA // MOBA_HEAD_DIM
    H, dh = MOBA_HEADS, MOBA_HEAD_DIM
    slopes = jnp.asarray(np.exp2(-8.0 * (np.arange(H, dtype=np.float64) + 1.0) / H), F32)
    return pl.pallas_call(
        _moba_kernel,
        grid=(batch, H, nb),
        in_specs=[pl.BlockSpec(memory_space=pltpu.SMEM),
                  pl.BlockSpec((CHUNK, dh), lambda b, h, i: (b * nb + i, base + h)),
                  pl.BlockSpec((seq, dh), lambda b, h, i: (b, base + H + h)),
                  pl.BlockSpec((seq, dh), lambda b, h, i: (b, base + 2 * H + h)),
                  pl.BlockSpec((CHUNK, dh), lambda b, h, i: (b * nb + i, base + 3 * H + h))],
        out_specs=pl.BlockSpec((CHUNK, dh), lambda b, h, i: (b * nb + i, h)),
        out_shape=jax.ShapeDtypeStruct((t, D_BRANCH), BF16),
        scratch_shapes=[pltpu.VMEM((LANES, dh), F32)],
        compiler_params=_params("parallel", "parallel", "arbitrary"),
        name="moba",
    )(slopes, proj, proj, proj, proj)


def _causal_conv(buf_ref, x, cw_ref, cb_ref, first):
    n = x.shape[0]

    @pl.when(first)
    def _():
        buf_ref[0:SUBLANES, :] = jnp.zeros((SUBLANES, buf_ref.shape[1]), F32)

    buf_ref[SUBLANES:SUBLANES + n, :] = x
    y = cb_ref[...]
    for k in range(CONV_WIDTH):
        off = SUBLANES - (CONV_WIDTH - 1) + k
        y = y + buf_ref[off:off + n, :] * cw_ref[k:k + 1, :]
    buf_ref[0:SUBLANES, :] = buf_ref[n:n + SUBLANES, :]
    return y


def _ssd_kernel(z_ref, xbc_ref, h_ref, wdt_ref, wdtT_ref, cw_ref, cb_ref, dtb_ref, dtbT_ref,
                alog_ref, alogT_ref, dexp_ref, nw_ref, e_ref, o_ref, buf_ref, state_ref):
    c = pl.program_id(1)
    Q, N, G = CHUNK, SSM_STATE, SSM_GROUPS
    R = SSM_HEADS // G
    GW = D_BRANCH // G

    @pl.when(c == 0)
    def _():
        state_ref[...] = jnp.zeros_like(state_ref)

    act = _silu(_causal_conv(buf_ref, xbc_ref[...], cw_ref, cb_ref, c == 0))
    xs = act[:, :D_BRANCH]
    bm = act[:, D_BRANCH:D_BRANCH + G * N]
    cm = act[:, D_BRANCH + G * N:]

    hb = h_ref[...]
    dt = _softplus(jnp.dot(hb, wdt_ref[...], preferred_element_type=F32) + dtb_ref[...])
    dtT = _softplus(lax.dot_general(wdtT_ref[...], hb, NT_DIMS, preferred_element_type=F32)
                    + dtbT_ref[...])
    da = dt * (-jnp.exp(alog_ref[...]))
    daT = dtT * (-jnp.exp(alogT_ref[...]))

    row = lax.broadcasted_iota(jnp.int32, (Q, Q), 0)
    col = lax.broadcasted_iota(jnp.int32, (Q, Q), 1)
    tri = row >= col
    lower = jnp.where(tri, 1.0, 0.0).astype(F32)
    upper = jnp.where(row <= col, 1.0, 0.0).astype(F32)
    da_cs = jnp.dot(lower, da, precision=HIGHEST, preferred_element_type=F32)
    da_csT = jnp.dot(daT, upper, precision=HIGHEST, preferred_element_type=F32)

    e = e_ref[...]
    dt_exp = jnp.dot(dt, e, precision=HIGHEST, preferred_element_type=F32)
    cs_exp = jnp.dot(da_cs, e, precision=HIGHEST, preferred_element_type=F32)
    last_exp = cs_exp[Q - 1:Q, :]
    xdt = xs * dt_exp
    x_w = xdt * jnp.exp(last_exp - cs_exp)
    chunk_decay = jnp.exp(last_exp)
    in_decay = jnp.exp(cs_exp)

    lane = lax.broadcasted_iota(jnp.int32, (Q, GW), 1)
    for g in range(G):
        gs = slice(g * GW, (g + 1) * GW)
        bg = bm[:, g * N:(g + 1) * N]
        cgb = cm[:, g * N:(g + 1) * N].astype(BF16)
        bgT = bg.T.astype(BF16)
        cb = jnp.dot(cgb, bgT, preferred_element_type=F32)
        st = state_ref[:, gs]
        y_off = jnp.dot(cgb, st.astype(BF16), preferred_element_type=F32)
        state_ref[:, gs] = st * chunk_decay[:, gs] + jnp.dot(
            bgT, x_w[:, gs].astype(BF16), preferred_element_type=F32)
        xdt_g = xdt[:, gs]
        w_parts, x_parts = [], []
        for r in range(R):
            hd = g * R + r
            seg = da_cs[:, hd:hd + 1] - da_csT[hd:hd + 1, :]
            lmat = jnp.exp(jnp.where(tri, seg, -jnp.inf))
            w_parts.append((cb * lmat).astype(BF16))
            in_head = jnp.logical_and(lane >= r * SSM_HEAD_DIM, lane < (r + 1) * SSM_HEAD_DIM)
            x_parts.append(jnp.where(in_head, xdt_g, 0.0).astype(BF16))
        y_diag = jnp.dot(jnp.concatenate(w_parts, axis=1), jnp.concatenate(x_parts, axis=0),
                         preferred_element_type=F32)
        y = y_diag + y_off * in_decay[:, gs] + dexp_ref[:, gs] * xs[:, gs]
        y = y * _silu(z_ref[:, gs])
        ms = jnp.mean(y * y, axis=-1, keepdims=True)
        o_ref[:, gs] = (y * lax.rsqrt(ms + NORM_EPS) * nw_ref[:, gs]).astype(o_ref.dtype)


def _ssd(proj, hnorm, w_dt, conv_w, conv_b, dt_bias, a_log, d_skip, norm_w, batch, seq):
    nc = seq // CHUNK
    t = batch * seq
    d = hnorm.shape[1]
    H = SSM_HEADS
    w_dt_pad = jnp.pad(w_dt, ((0, 0), (0, LANES - H))).astype(BF16)
    w_dtT = w_dt.T.astype(BF16)
    pad = lambda v: jnp.pad(v.reshape(1, H), ((0, 0), (0, LANES - H)))
    expand = np.zeros((LANES, D_BRANCH), np.float32)
    for hd in range(H):
        expand[hd, hd * SSM_HEAD_DIM:(hd + 1) * SSM_HEAD_DIM] = 1.0
    d_exp = jnp.repeat(d_skip, SSM_HEAD_DIM).reshape(1, D_BRANCH)
    const = lambda shape: pl.BlockSpec(shape, lambda b, c: (0,) * len(shape))
    return pl.pallas_call(
        _ssd_kernel,
        grid=(batch, nc),
        in_specs=[pl.BlockSpec((CHUNK, D_BRANCH), lambda b, c: (b * nc + c, COL_Z // D_BRANCH)),
                  pl.BlockSpec((CHUNK, SSM_CONV_DIM), lambda b, c: (b * nc + c, COL_XBC // SSM_CONV_DIM)),
                  pl.BlockSpec((CHUNK, d), lambda b, c: (b * nc + c, 0)),
                  const((d, LANES)), const((H, d)),
                  const((CONV_WIDTH, SSM_CONV_DIM)), const((1, SSM_CONV_DIM)),
                  const((1, LANES)), const((H, 1)), const((1, LANES)), const((H, 1)),
                  const((1, D_BRANCH)), const((1, D_BRANCH)), const((LANES, D_BRANCH))],
        out_specs=pl.BlockSpec((CHUNK, D_BRANCH), lambda b, c: (b * nc + c, 0)),
        out_shape=jax.ShapeDtypeStruct((t, D_BRANCH), BF16),
        scratch_shapes=[pltpu.VMEM((CHUNK + 2 * SUBLANES, SSM_CONV_DIM), F32),
                        pltpu.VMEM((SSM_STATE, D_BRANCH), F32)],
        compiler_params=_params("parallel", "arbitrary"),
        name="ssd",
    )(proj, proj, hnorm, w_dt_pad, w_dtT, conv_w, conv_b.reshape(1, -1),
      pad(dt_bias), dt_bias.reshape(H, 1), pad(a_log), a_log.reshape(H, 1),
      d_exp, norm_w.reshape(1, D_BRANCH), jnp.asarray(expand))


def _lru_kernel(x_ref, g_ref, cw_ref, cb_ref, wa_ref, ba_ref, wx_ref, bx_ref, lam_ref,
                o_ref, buf_ref, carry_ref):
    c = pl.program_id(1)
    n = x_ref.shape[0]

    @pl.when(c == 0)
    def _():
        carry_ref[...] = jnp.zeros_like(carry_ref)

    xc = _causal_conv(buf_ref, x_ref[...], cw_ref, cb_ref, c == 0)
    ra, ix = [], []
    for blk in range(LRU_BLOCKS):
        xb = xc[:, blk * LRU_BLOCK_DIM:(blk + 1) * LRU_BLOCK_DIM].astype(BF16)
        ra.append(jnp.dot(xb, wa_ref[blk], preferred_element_type=F32))
        ix.append(jnp.dot(xb, wx_ref[blk], preferred_element_type=F32))
    r = _sigmoid(jnp.concatenate(ra, axis=1) + ba_ref[...])
    i = _sigmoid(jnp.concatenate(ix, axis=1) + bx_ref[...])
    log_a = -LRU_C * r * _softplus(-lam_ref[...])
    a = jnp.exp(log_a)
    u = jnp.sqrt(-jnp.tanh(log_a) * (1.0 + a * a)) * (i * xc)

    row = lax.broadcasted_iota(jnp.int32, a.shape, 0)
    d = 1
    while d < n:
        keep = row >= d
        a_sh = jnp.where(keep, pltpu.roll(a, d, 0), 1.0)
        u_sh = jnp.where(keep, pltpu.roll(u, d, 0), 0.0)
        u = a * u_sh + u
        a = a * a_sh
        d *= 2
    hseq = u + a * carry_ref[...]
    carry_ref[...] = hseq[n - 1:n, :]
    o_ref[...] = (hseq * _silu(g_ref[...])).astype(o_ref.dtype)


def _lru(proj, conv_w, conv_b, w_a, b_a, w_x, b_x, lam, batch, seq):
    lc = CHUNK
    nc = seq // lc
    t = batch * seq
    const = lambda shape: pl.BlockSpec(shape, lambda b, c: (0,) * len(shape))
    wshape = (LRU_BLOCKS, LRU_BLOCK_DIM, LRU_BLOCK_DIM)
    return pl.pallas_call(
        _lru_kernel,
        grid=(batch, nc),
        in_specs=[pl.BlockSpec((lc, D_BRANCH), lambda b, c: (b * nc + c, COL_LRU_X // D_BRANCH)),
                  pl.BlockSpec((lc, D_BRANCH), lambda b, c: (b * nc + c, COL_LRU_G // D_BRANCH)),
                  const((CONV_WIDTH, D_BRANCH)), const((1, D_BRANCH)),
                  const(wshape), const((1, D_BRANCH)), const(wshape), const((1, D_BRANCH)),
                  const((1, D_BRANCH))],
        out_specs=pl.BlockSpec((lc, D_BRANCH), lambda b, c: (b * nc + c, 0)),
        out_shape=jax.ShapeDtypeStruct((t, D_BRANCH), BF16),
        scratch_shapes=[pltpu.VMEM((lc + 2 * SUBLANES, D_BRANCH), F32),
                        pltpu.VMEM((1, D_BRANCH), F32)],
        compiler_params=_params("parallel", "arbitrary"),
        name="rglru",
    )(proj, proj, conv_w, conv_b.reshape(1, -1), w_a.astype(BF16), b_a.reshape(1, -1),
      w_x.astype(BF16), b_x.reshape(1, -1), lam.reshape(1, -1))


def _outproj_kernel(y0_ref, y1_ref, y2_ref, y3_ref, w_ref, x_ref, o_ref):
    acc = x_ref[...]
    for j, y_ref in enumerate((y0_ref, y1_ref, y2_ref, y3_ref)):
        acc = acc + jnp.dot(y_ref[...], w_ref[j * D_BRANCH:(j + 1) * D_BRANCH, :],
                            preferred_element_type=F32)
    o_ref[...] = acc


def _outproj(ys, w, x):
    t, d = x.shape
    k = w.shape[0]
    tm = min(512, t)
    tn = 1024
    yspec = pl.BlockSpec((tm, D_BRANCH), lambda j, i: (i, 0))
    return pl.pallas_call(
        _outproj_kernel,
        grid=(d // tn, t // tm),
        in_specs=[yspec, yspec, yspec, yspec,
                  pl.BlockSpec((k, tn), lambda j, i: (0, j)),
                  pl.BlockSpec((tm, tn), lambda j, i: (i, j))],
        out_specs=pl.BlockSpec((tm, tn), lambda j, i: (i, j)),
        out_shape=jax.ShapeDtypeStruct((t, d), F32),
        compiler_params=_params("parallel", "parallel"),
        name="outproj",
    )(*ys, w, x)


def kernel(x, norm_w, w_in, ret_gn_w, ssm_conv_w, ssm_conv_b, ssm_dt_bias, ssm_a_log, ssm_d, ssm_norm_w, lru_conv_w, lru_conv_b, lru_w_a, lru_b_a, lru_w_x, lru_b_x, lru_lambda, w_out, final_norm_w):
    batch, seq, d = x.shape
    depth = w_in.shape[0]
    assert seq % CHUNK == 0 and d == 4 * D_BRANCH
    t = batch * seq
    o_z = 8 * D_BRANCH
    o_xbc = o_z + D_BRANCH
    o_dt = o_xbc + SSM_CONV_DIM
    o_lru = o_dt + SSM_HEADS
    xf = x.reshape(t, d)
    for l in range(depth):
        wl = w_in[l]
        w_cat = jnp.concatenate([wl[:, :o_z], wl[:, o_xbc:o_dt], wl[:, o_z:o_xbc], wl[:, o_lru:]],
                                axis=1).astype(BF16)
        hn = _rmsnorm(xf, norm_w[l], BF16)
        proj = _inproj(hn, w_cat)
        y_ret = _retention(proj, ret_gn_w[l], batch, seq)
        y_moba = _moba(proj, batch, seq)
        y_ssm = _ssd(proj, hn, wl[:, o_dt:o_lru], ssm_conv_w[l], ssm_conv_b[l], ssm_dt_bias[l],
                     ssm_a_log[l], ssm_d[l], ssm_norm_w[l], batch, seq)
        y_lru = _lru(proj, lru_conv_w[l], lru_conv_b[l], lru_w_a[l], lru_b_a[l], lru_w_x[l],
                     lru_b_x[l], lru_lambda[l], batch, seq)
        xf = _outproj((y_ret, y_moba, y_ssm, y_lru), w_out[l].astype(BF16), xf)
    return _rmsnorm(xf, final_norm_w, F32).reshape(batch, seq, d)
```

```python
import math

import jax
import jax.numpy as jnp
import numpy as np
from jax import lax
from jax.experimental import pallas as pl
from jax.experimental.pallas import tpu as pltpu

F32 = jnp.float32
BF16 = jnp.bfloat16
HIGHEST = lax.Precision.HIGHEST

NORM_EPS = 1e-6
D_BRANCH = 1024
CHUNK = 256
RET_HEADS = 4
RET_HEAD_DIM = 256
MOBA_HEADS = 8
MOBA_HEAD_DIM = 128
MOBA_TOPK = 3
MOBA_HEADS_PER_STEP = 8
SSM_HEADS = 16
SSM_HEAD_DIM = 64
SSM_GROUPS = 4
SSM_STATE = 128
SSM_CONV_DIM = 2048
CONV_WIDTH = 4
LRU_BLOCKS = 8
LRU_BLOCK_DIM = 128
LRU_C = 8.0
LANES = 128
SUBLANES = 8
VMEM_LIMIT = 56 * 1024 * 1024

COL_RET = 0
COL_MOBA = 4 * D_BRANCH
COL_Z = 8 * D_BRANCH
COL_XS = COL_Z + D_BRANCH
COL_BC = COL_XS + D_BRANCH
COL_DT = COL_XS + SSM_CONV_DIM
COL_LRU = COL_DT + SSM_HEADS
D_MAIN = COL_DT

NT_DIMS = (((1,), (1,)), ((), ()))


def _params(*sem):
    return pltpu.CompilerParams(dimension_semantics=sem, vmem_limit_bytes=VMEM_LIMIT)


def _sigmoid(x):
    return 1.0 / (1.0 + jnp.exp(-x))


def _silu(x):
    return x * _sigmoid(x)


def _softplus(x):
    return jnp.maximum(x, 0.0) + jnp.log1p(jnp.exp(-jnp.abs(x)))


def _rmsnorm_kernel(x_ref, w_ref, o_ref):
    x = x_ref[...]
    ms = jnp.mean(x * x, axis=-1, keepdims=True)
    o_ref[...] = (x * lax.rsqrt(ms + NORM_EPS) * w_ref[...]).astype(o_ref.dtype)


def _rmsnorm(x, w, out_dtype):
    t, d = x.shape
    tm = min(512, t)
    return pl.pallas_call(
        _rmsnorm_kernel,
        grid=(t // tm,),
        in_specs=[pl.BlockSpec((tm, d), lambda i: (i, 0)),
                  pl.BlockSpec((1, d), lambda i: (0, 0))],
        out_specs=pl.BlockSpec((tm, d), lambda i: (i, 0)),
        out_shape=jax.ShapeDtypeStruct((t, d), out_dtype),
        compiler_params=_params("parallel"),
        name="rmsnorm",
    )(x, w.reshape(1, d))


def _inproj_kernel(a_ref, w_ref, o_ref, wb_ref):
    @pl.when(pl.program_id(1) == 0)
    def _():
        wb_ref[...] = w_ref[...].astype(BF16)

    o_ref[...] = jnp.dot(a_ref[...], wb_ref[...], preferred_element_type=F32).astype(o_ref.dtype)


def _inproj(h, w, layer, n):
    t, k = h.shape
    tm = min(1024, t)
    tn = 512
    return pl.pallas_call(
        _inproj_kernel,
        grid=(n // tn, t // tm),
        in_specs=[pl.BlockSpec((tm, k), lambda j, i: (i, 0)),
                  pl.BlockSpec((None, k, tn), lambda j, i: (layer, 0, j))],
        out_specs=pl.BlockSpec((tm, tn), lambda j, i: (i, j)),
        out_shape=jax.ShapeDtypeStruct((t, n), BF16),
        scratch_shapes=[pltpu.VMEM((k, tn), BF16)],
        compiler_params=_params("parallel", "arbitrary"),
        name="inproj",
    )(h, w)


def _retention_kernel(q_ref, k_ref, v_ref, g_ref, gn_ref, o_ref, state_ref):
    c = pl.program_id(1)

    @pl.when(c == 0)
    def _():
        state_ref[...] = jnp.zeros_like(state_ref)

    C, dh = CHUNK, RET_HEAD_DIM
    row = lax.broadcasted_iota(jnp.int32, (C, C), 0)
    col = lax.broadcasted_iota(jnp.int32, (C, C), 1)
    diff = (row - col).astype(F32)
    rowf = row.astype(F32)
    causal = row >= col
    for h in range(RET_HEADS):
        log_g = math.log1p(-(2.0 ** (-5.0 - h)))
        sl = slice(h * dh, (h + 1) * dh)
        dmat = jnp.where(causal, jnp.exp(log_g * jnp.where(causal, diff, 0.0)), 0.0)
        q_decay = jnp.exp(log_g * (rowf + 1.0))
        k_decay = jnp.exp(log_g * (C - 1.0 - rowf))
        qb = q_ref[:, sl]
        k = k_ref[:, sl].astype(F32) * (dh ** -0.5)
        kb = k.astype(BF16)
        vb = v_ref[:, sl]
        state = state_ref[h]
        inner = lax.dot_general(qb, kb, NT_DIMS, preferred_element_type=F32) * dmat
        y = jnp.dot(inner.astype(BF16), vb, preferred_element_type=F32)
        y = y + jnp.dot(qb, state.astype(BF16), preferred_element_type=F32) * q_decay
        kdT = (k * k_decay).T.astype(BF16)
        state_ref[h] = math.exp(log_g * C) * state + jnp.dot(kdT, vb, preferred_element_type=F32)
        ms = jnp.mean(y * y, axis=-1, keepdims=True)
        y = y * lax.rsqrt(ms + NORM_EPS) * gn_ref[h]
        o_ref[:, sl] = (y * _silu(g_ref[:, sl].astype(F32))).astype(o_ref.dtype)


def _retention(proj, gn_w, batch, seq):
    nc = seq // CHUNK
    t = batch * seq
    base = COL_RET // D_BRANCH

    def spec(j):
        return pl.BlockSpec((CHUNK, D_BRANCH), lambda b, c, j=j: (b * nc + c, base + j))

    return pl.pallas_call(
        _retention_kernel,
        grid=(batch, nc),
        in_specs=[spec(0), spec(1), spec(2), spec(3),
                  pl.BlockSpec((RET_HEADS, 1, RET_HEAD_DIM), lambda b, c: (0, 0, 0))],
        out_specs=pl.BlockSpec((CHUNK, D_BRANCH), lambda b, c: (b * nc + c, 0)),
        out_shape=jax.ShapeDtypeStruct((t, D_BRANCH), BF16),
        scratch_shapes=[pltpu.VMEM((RET_HEADS, RET_HEAD_DIM, RET_HEAD_DIM), F32)],
        compiler_params=_params("parallel", "arbitrary"),
        name="retention",
    )(proj, proj, proj, proj, gn_w.reshape(RET_HEADS, 1, RET_HEAD_DIM))


MOBA_SPLIT = 3
MOBA_VT_ROWS = MOBA_HEAD_DIM + 16
LOG2E = math.log2(math.e)


def _split_bf16(x):
    terms = []
    for _ in range(MOBA_SPLIT - 1):
        hi = x.astype(BF16).astype(F32)
        terms.append(hi)
        x = x - hi
    return terms + [x]


def _moba_kernel(slopes_ref, q_ref, k_ref, v_ref, g_ref, o_ref, kmean_ref, kx_ref, vt_ref, sel_ref):
    hg = pl.program_id(1)
    i = pl.program_id(2)
    L, dh, HP = CHUNK, MOBA_HEAD_DIM, MOBA_HEADS_PER_STEP
    seq = k_ref.shape[0]
    nb = seq // L
    nbp = kmean_ref.shape[1]
    neg = -jnp.inf
    lane = lax.broadcasted_iota(jnp.int32, (L, LANES), 1)

    @pl.when(i == 0)
    def _():
        kmean_ref[...] = jnp.zeros_like(kmean_ref)
        vt_ref[:, :, dh:, :] = jnp.ones((HP, nb, MOBA_VT_ROWS - dh, L), BF16)
        within = lax.broadcasted_iota(jnp.int32, (L, LANES), 0).astype(F32)
        for n in range(nb):
            kx = jnp.where(lane < MOBA_SPLIT, within,
                           jnp.where(lane < 2 * MOBA_SPLIT, float(n), 0.0))
            kx_ref[n * L:(n + 1) * L, :] = kx.astype(BF16)
            for hh in range(HP):
                hs = slice(hh * dh, (hh + 1) * dh)
                kn = k_ref[n * L:(n + 1) * L, hs].astype(F32)
                kmean_ref[hh, n:n + 1, :] = jnp.mean(kn, axis=0, keepdims=True)
                vn = v_ref[n * L:(n + 1) * L, hs].astype(F32)
                vt_ref[hh, n, 0:dh, :] = vn.T.astype(BF16)

    blk_row = lax.broadcasted_iota(jnp.int32, (nbp, L), 0)
    blk_rowf = blk_row.astype(F32)
    key = lax.broadcasted_iota(jnp.int32, (L, L), 0)
    qry = lax.broadcasted_iota(jnp.int32, (L, L), 1)
    lane1 = lax.broadcasted_iota(jnp.int32, (1, LANES), 1)
    own = pl.multiple_of(i * L, L)

    def scores(hh, qa, off):
        hs = slice(hh * dh, (hh + 1) * dh)
        ka = jnp.concatenate([k_ref[pl.ds(off, L), hs], kx_ref[pl.ds(off, L), :]], axis=1)
        return lax.dot_general(ka, qa, NT_DIMS, preferred_element_type=F32)

    qas, carry0 = [], []
    for hh in range(HP):
        hs = slice(hh * dh, (hh + 1) * dh)
        q = q_ref[:, hs].astype(F32)
        gate = lax.dot_general(kmean_ref[hh], q, NT_DIMS, precision=HIGHEST,
                               preferred_element_type=F32)
        gm = jnp.where(blk_row < i, gate, neg)
        sel = jnp.zeros(gate.shape, F32)
        for _ in range(MOBA_TOPK):
            mx = jnp.max(gm, axis=0, keepdims=True)
            first = jnp.min(jnp.where(gm == mx, blk_rowf, float(nbp)), axis=0, keepdims=True)
            pick = jnp.logical_and(blk_rowf == first, mx > neg)
            sel = jnp.where(pick, 1.0, sel)
            gm = jnp.where(pick, neg, gm)
        sel_ref[hh] = sel
        slope2 = jnp.full((1, LANES), slopes_ref[hg * HP + hh] * LOG2E, F32)
        qx = jnp.zeros((1, LANES), F32)
        for t, term in enumerate(_split_bf16(slope2) + _split_bf16(slope2 * L)):
            qx = jnp.where(lane1 == t, term, qx)
        qa = jnp.concatenate([(q * (dh ** -0.5 * LOG2E)).astype(BF16),
                              jnp.broadcast_to(qx, (L, LANES)).astype(BF16)], axis=1)
        qas.append(qa)
        s = jnp.where(key <= qry, scores(hh, qa, own), neg)
        m0 = jnp.max(s, axis=0, keepdims=True)
        p = jnp.exp2(s - m0)
        carry0.append((m0, jnp.dot(vt_ref[hh, i], p.astype(BF16), preferred_element_type=F32)))

    def body(j, carry):
        off = pl.multiple_of(j * L, L)
        new = []
        for hh in range(HP):
            m, acc = carry[hh]
            picked = sel_ref[hh, pl.ds(j, 1), :] > 0.0
            s = scores(hh, qas[hh], off)
            m_new = jnp.maximum(m, jnp.where(picked, jnp.max(s, axis=0, keepdims=True), neg))
            p = jnp.exp2(s - jnp.where(picked, m_new, jnp.inf))
            acc = acc * jnp.exp2(m - m_new) + jnp.dot(vt_ref[hh, j], p.astype(BF16),
                                                      preferred_element_type=F32)
            new.append((m_new, acc))
        return tuple(new)

    final = lax.fori_loop(0, i, body, tuple(carry0))
    for hh in range(HP):
        hs = slice(hh * dh, (hh + 1) * dh)
        acc = final[hh][1]
        out = (acc[:dh, :] / acc[dh:dh + 1, :]).T
        o_ref[:, hs] = (out * _silu(g_ref[:, hs].astype(F32))).astype(o_ref.dtype)


def _moba(proj, batch, seq):
    nb = seq // CHUNK
    t = batch * seq
    H, dh, HP = MOBA_HEADS, MOBA_HEAD_DIM, MOBA_HEADS_PER_STEP
    nbp = -(-nb // SUBLANES) * SUBLANES
    w = HP * dh
    base = COL_MOBA // w
    per = D_BRANCH // w
    slopes = jnp.asarray(np.exp2(-8.0 * (np.arange(H, dtype=np.float64) + 1.0) / H), F32)
    once = pl.Buffered(1)
    return pl.pallas_call(
        _moba_kernel,
        grid=(batch, H // HP, nb),
        in_specs=[pl.BlockSpec(memory_space=pltpu.SMEM),
                  pl.BlockSpec((CHUNK, w), lambda b, h, i: (b * nb + i, base + h)),
                  pl.BlockSpec((seq, w), lambda b, h, i: (b, base + per + h), pipeline_mode=once),
                  pl.BlockSpec((seq, w), lambda b, h, i: (b, base + 2 * per + h), pipeline_mode=once),
                  pl.BlockSpec((CHUNK, w), lambda b, h, i: (b * nb + i, base + 3 * per + h))],
        out_specs=pl.BlockSpec((CHUNK, w), lambda b, h, i: (b * nb + i, h)),
        out_shape=jax.ShapeDtypeStruct((t, D_BRANCH), BF16),
        scratch_shapes=[pltpu.VMEM((HP, nbp, dh), F32),
                        pltpu.VMEM((seq, LANES), BF16),
                        pltpu.VMEM((HP, nb, MOBA_VT_ROWS, CHUNK), BF16),
                        pltpu.VMEM((HP, nbp, CHUNK), F32)],
        compiler_params=_params("parallel", "parallel", "arbitrary"),
        name="moba",
    )(slopes, proj, proj, proj, proj)


def _causal_conv(buf_ref, parts, cw_ref, cb_ref, first):
    n = parts[0].shape[0]

    @pl.when(first)
    def _():
        buf_ref[0:SUBLANES, :] = jnp.zeros((SUBLANES, buf_ref.shape[1]), F32)

    c0 = 0
    for part in parts:
        buf_ref[SUBLANES:SUBLANES + n, c0:c0 + part.shape[1]] = part.astype(F32)
        c0 += part.shape[1]
    y = cb_ref[...]
    for k in range(CONV_WIDTH):
        off = SUBLANES - (CONV_WIDTH - 1) + k
        y = y + buf_ref[off:off + n, :] * cw_ref[k:k + 1, :]
    buf_ref[0:SUBLANES, :] = buf_ref[n:n + SUBLANES, :]
    return y


def _ssd_kernel(z_ref, xs_ref, bc_ref, h_ref, wdt_ref, wdtT_ref, cw_ref, cb_ref, dtb_ref, dtbT_ref,
                alog_ref, alogT_ref, dexp_ref, nw_ref, e_ref, o_ref, buf_ref, state_ref):
    c = pl.program_id(1)
    Q, N, G = CHUNK, SSM_STATE, SSM_GROUPS
    R = SSM_HEADS // G
    GW = D_BRANCH // G

    @pl.when(c == 0)
    def _():
        state_ref[...] = jnp.zeros_like(state_ref)

    act = _silu(_causal_conv(buf_ref, [xs_ref[...], bc_ref[...]], cw_ref, cb_ref, c == 0))
    xs = act[:, :D_BRANCH]
    bm = act[:, D_BRANCH:D_BRANCH + G * N]
    cm = act[:, D_BRANCH + G * N:]

    hb = h_ref[...]
    dt = _softplus(jnp.dot(hb, wdt_ref[...], preferred_element_type=F32) + dtb_ref[...])
    dtT = _softplus(lax.dot_general(wdtT_ref[...], hb, NT_DIMS, preferred_element_type=F32)
                    + dtbT_ref[...])
    da = dt * (-jnp.exp(alog_ref[...]))
    daT = dtT * (-jnp.exp(alogT_ref[...]))

    row = lax.broadcasted_iota(jnp.int32, (Q, Q), 0)
    col = lax.broadcasted_iota(jnp.int32, (Q, Q), 1)
    tri = row >= col
    lower = jnp.where(tri, 1.0, 0.0).astype(F32)
    upper = jnp.where(row <= col, 1.0, 0.0).astype(F32)
    da_cs = jnp.dot(lower, da, precision=HIGHEST, preferred_element_type=F32)
    da_csT = jnp.dot(daT, upper, precision=HIGHEST, preferred_element_type=F32)

    e = e_ref[...]
    dt_exp = jnp.dot(dt, e, precision=HIGHEST, preferred_element_type=F32)
    cs_exp = jnp.dot(da_cs, e, precision=HIGHEST, preferred_element_type=F32)
    last_exp = cs_exp[Q - 1:Q, :]
    xdt = xs * dt_exp
    x_w = xdt * jnp.exp(last_exp - cs_exp)
    chunk_decay = jnp.exp(last_exp)
    in_decay = jnp.exp(cs_exp)

    lane = lax.broadcasted_iota(jnp.int32, (Q, GW), 1)
    for g in range(G):
        gs = slice(g * GW, (g + 1) * GW)
        bg = bm[:, g * N:(g + 1) * N]
        cgb = cm[:, g * N:(g + 1) * N].astype(BF16)
        bgT = bg.T.astype(BF16)
        cb = jnp.dot(cgb, bgT, preferred_element_type=F32)
        st = state_ref[:, gs]
        y_off = jnp.dot(cgb, st.astype(BF16), preferred_element_type=F32)
        state_ref[:, gs] = st * chunk_decay[:, gs] + jnp.dot(
            bgT, x_w[:, gs].astype(BF16), preferred_element_type=F32)
        xdt_g = xdt[:, gs]
        w_parts, x_parts = [], []
        for r in range(R):
            hd = g * R + r
            seg = da_cs[:, hd:hd + 1] - da_csT[hd:hd + 1, :]
            lmat = jnp.exp(jnp.where(tri, seg, -jnp.inf))
            w_parts.append((cb * lmat).astype(BF16))
            in_head = jnp.logical_and(lane >= r * SSM_HEAD_DIM, lane < (r + 1) * SSM_HEAD_DIM)
            x_parts.append(jnp.where(in_head, xdt_g, 0.0).astype(BF16))
        y_diag = jnp.dot(jnp.concatenate(w_parts, axis=1), jnp.concatenate(x_parts, axis=0),
                         preferred_element_type=F32)
        y = y_diag + y_off * in_decay[:, gs] + dexp_ref[:, gs] * xs[:, gs]
        y = y * _silu(z_ref[:, gs].astype(F32))
        ms = jnp.mean(y * y, axis=-1, keepdims=True)
        o_ref[:, gs] = (y * lax.rsqrt(ms + NORM_EPS) * nw_ref[:, gs]).astype(o_ref.dtype)


def _ssd(proj, hnorm, w_dt, conv_w, conv_b, dt_bias, a_log, d_skip, norm_w, batch, seq):
    nc = seq // CHUNK
    t = batch * seq
    d = hnorm.shape[1]
    H = SSM_HEADS
    w_dt_pad = jnp.pad(w_dt, ((0, 0), (0, LANES - H))).astype(BF16)
    w_dtT = w_dt.T.astype(BF16)
    pad = lambda v: jnp.pad(v.reshape(1, H), ((0, 0), (0, LANES - H)))
    expand = np.zeros((LANES, D_BRANCH), np.float32)
    for hd in range(H):
        expand[hd, hd * SSM_HEAD_DIM:(hd + 1) * SSM_HEAD_DIM] = 1.0
    d_exp = jnp.repeat(d_skip, SSM_HEAD_DIM).reshape(1, D_BRANCH)
    const = lambda shape: pl.BlockSpec(shape, lambda b, c: (0,) * len(shape))
    blk = lambda col0: pl.BlockSpec((CHUNK, D_BRANCH), lambda b, c: (b * nc + c, col0 // D_BRANCH))
    return pl.pallas_call(
        _ssd_kernel,
        grid=(batch, nc),
        in_specs=[blk(COL_Z), blk(COL_XS), blk(COL_BC),
                  pl.BlockSpec((CHUNK, d), lambda b, c: (b * nc + c, 0)),
                  const((d, LANES)), const((H, d)),
                  const((CONV_WIDTH, SSM_CONV_DIM)), const((1, SSM_CONV_DIM)),
                  const((1, LANES)), const((H, 1)), const((1, LANES)), const((H, 1)),
                  const((1, D_BRANCH)), const((1, D_BRANCH)), const((LANES, D_BRANCH))],
        out_specs=pl.BlockSpec((CHUNK, D_BRANCH), lambda b, c: (b * nc + c, 0)),
        out_shape=jax.ShapeDtypeStruct((t, D_BRANCH), BF16),
        scratch_shapes=[pltpu.VMEM((CHUNK + 2 * SUBLANES, SSM_CONV_DIM), F32),
                        pltpu.VMEM((SSM_STATE, D_BRANCH), F32)],
        compiler_params=_params("parallel", "arbitrary"),
        name="ssd",
    )(proj, proj, proj, hnorm, w_dt_pad, w_dtT, conv_w, conv_b.reshape(1, -1),
      pad(dt_bias), dt_bias.reshape(H, 1), pad(a_log), a_log.reshape(H, 1),
      d_exp, norm_w.reshape(1, D_BRANCH), jnp.asarray(expand))


def _lru_kernel(x_ref, g_ref, cw_ref, cb_ref, wa_ref, ba_ref, wx_ref, bx_ref, lam_ref,
                o_ref, buf_ref, carry_ref):
    c = pl.program_id(1)
    n = x_ref.shape[0]

    @pl.when(c == 0)
    def _():
        carry_ref[...] = jnp.zeros_like(carry_ref)

    xc = _causal_conv(buf_ref, [x_ref[...]], cw_ref, cb_ref, c == 0)
    ra, ix = [], []
    for blk in range(LRU_BLOCKS):
        xb = xc[:, blk * LRU_BLOCK_DIM:(blk + 1) * LRU_BLOCK_DIM].astype(BF16)
        ra.append(jnp.dot(xb, wa_ref[blk], preferred_element_type=F32))
        ix.append(jnp.dot(xb, wx_ref[blk], preferred_element_type=F32))
    r = _sigmoid(jnp.concatenate(ra, axis=1) + ba_ref[...])
    i = _sigmoid(jnp.concatenate(ix, axis=1) + bx_ref[...])
    log_a = -LRU_C * r * _softplus(-lam_ref[...])
    a = jnp.exp(log_a)
    u = jnp.sqrt(-jnp.tanh(log_a) * (1.0 + a * a)) * (i * xc)

    row = lax.broadcasted_iota(jnp.int32, a.shape, 0)
    d = 1
    while d < n:
        keep = row >= d
        a_sh = jnp.where(keep, pltpu.roll(a, d, 0), 1.0)
        u_sh = jnp.where(keep, pltpu.roll(u, d, 0), 0.0)
        u = a * u_sh + u
        a = a * a_sh
        d *= 2
    hseq = u + a * carry_ref[...]
    carry_ref[...] = hseq[n - 1:n, :]
    o_ref[...] = (hseq * _silu(g_ref[...].astype(F32))).astype(o_ref.dtype)


def _lru(proj, conv_w, conv_b, w_a, b_a, w_x, b_x, lam, batch, seq):
    lc = CHUNK
    nc = seq // lc
    t = batch * seq
    const = lambda shape: pl.BlockSpec(shape, lambda b, c: (0,) * len(shape))
    wshape = (LRU_BLOCKS, LRU_BLOCK_DIM, LRU_BLOCK_DIM)
    return pl.pallas_call(
        _lru_kernel,
        grid=(batch, nc),
        in_specs=[pl.BlockSpec((lc, D_BRANCH), lambda b, c: (b * nc + c, 0)),
                  pl.BlockSpec((lc, D_BRANCH), lambda b, c: (b * nc + c, 1)),
                  const((CONV_WIDTH, D_BRANCH)), const((1, D_BRANCH)),
                  const(wshape), const((1, D_BRANCH)), const(wshape), const((1, D_BRANCH)),
                  const((1, D_BRANCH))],
        out_specs=pl.BlockSpec((lc, D_BRANCH), lambda b, c: (b * nc + c, 0)),
        out_shape=jax.ShapeDtypeStruct((t, D_BRANCH), BF16),
        scratch_shapes=[pltpu.VMEM((lc + 2 * SUBLANES, D_BRANCH), F32),
                        pltpu.VMEM((1, D_BRANCH), F32)],
        compiler_params=_params("parallel", "arbitrary"),
        name="rglru",
    )(proj, proj, conv_w, conv_b.reshape(1, -1), w_a.astype(BF16), b_a.reshape(1, -1),
      w_x.astype(BF16), b_x.reshape(1, -1), lam.reshape(1, -1))


def _outproj_kernel(y0_ref, y1_ref, y2_ref, y3_ref, w_ref, x_ref, o_ref, wb_ref):
    @pl.when(pl.program_id(1) == 0)
    def _():
        wb_ref[...] = w_ref[...].astype(BF16)

    acc = x_ref[...]
    for j, y_ref in enumerate((y0_ref, y1_ref, y2_ref, y3_ref)):
        acc = acc + jnp.dot(y_ref[...], wb_ref[j * D_BRANCH:(j + 1) * D_BRANCH, :],
                            preferred_element_type=F32)
    o_ref[...] = acc


def _outproj(ys, w, layer, x):
    t, d = x.shape
    k = w.shape[1]
    tm = min(512, t)
    tn = 512
    yspec = pl.BlockSpec((tm, D_BRANCH), lambda j, i: (i, 0))
    return pl.pallas_call(
        _outproj_kernel,
        grid=(d // tn, t // tm),
        in_specs=[yspec, yspec, yspec, yspec,
                  pl.BlockSpec((None, k, tn), lambda j, i: (layer, 0, j)),
                  pl.BlockSpec((tm, tn), lambda j, i: (i, j))],
        out_specs=pl.BlockSpec((tm, tn), lambda j, i: (i, j)),
        out_shape=jax.ShapeDtypeStruct((t, d), F32),
        scratch_shapes=[pltpu.VMEM((k, tn), BF16)],
        compiler_params=_params("parallel", "arbitrary"),
        name="outproj",
    )(*ys, w, x)


def kernel(x, norm_w, w_in, ret_gn_w, ssm_conv_w, ssm_conv_b, ssm_dt_bias, ssm_a_log, ssm_d, ssm_norm_w, lru_conv_w, lru_conv_b, lru_w_a, lru_b_a, lru_w_x, lru_b_x, lru_lambda, w_out, final_norm_w):
    batch, seq, d = x.shape
    depth = w_in.shape[0]
    assert seq % CHUNK == 0 and d == 4 * D_BRANCH
    t = batch * seq
    xf = x.reshape(t, d)
    for l in range(depth):
        hn = _rmsnorm(xf, norm_w[l], BF16)
        proj = _inproj(hn, w_in, l, D_MAIN)
        proj_lru = _inproj(hn, w_in[l:l + 1, :, COL_LRU:], 0, 2 * D_BRANCH)
        y_ret = _retention(proj, ret_gn_w[l], batch, seq)
        y_moba = _moba(proj, batch, seq)
        y_ssm = _ssd(proj, hn, w_in[l, :, COL_DT:COL_LRU], ssm_conv_w[l], ssm_conv_b[l],
                     ssm_dt_bias[l], ssm_a_log[l], ssm_d[l], ssm_norm_w[l], batch, seq)
        y_lru = _lru(proj_lru, lru_conv_w[l], lru_conv_b[l], lru_w_a[l], lru_b_a[l], lru_w_x[l],
                     lru_b_x[l], lru_lambda[l], batch, seq)
        xf = _outproj((y_ret, y_moba, y_ssm, y_lru), w_out, l, xf)
    return _rmsnorm(xf, final_norm_w, F32).reshape(batch, seq, d)
```

```python
import functools
import math

import jax
import jax.numpy as jnp
import numpy as np
from jax import lax
from jax.experimental import pallas as pl
from jax.experimental.pallas import tpu as pltpu

F32 = jnp.float32
BF16 = jnp.bfloat16
HIGHEST = lax.Precision.HIGHEST

NORM_EPS = 1e-6
D_BRANCH = 1024
CHUNK = 256
RET_HEADS = 4
RET_HEAD_DIM = 256
MOBA_HEADS = 8
MOBA_HEAD_DIM = 128
MOBA_TOPK = 3
MOBA_HEADS_PER_STEP = 8
SSM_HEADS = 16
SSM_HEAD_DIM = 64
SSM_GROUPS = 4
SSM_STATE = 128
SSM_CONV_DIM = 2048
CONV_WIDTH = 4
LRU_BLOCKS = 8
LRU_BLOCK_DIM = 128
LRU_C = 8.0
LANES = 128
SUBLANES = 8
VMEM_LIMIT = 56 * 1024 * 1024

COL_RET = 0
COL_MOBA = 4 * D_BRANCH
COL_Z = 8 * D_BRANCH
COL_XS = COL_Z + D_BRANCH
COL_BC = COL_XS + D_BRANCH
COL_DT = COL_XS + SSM_CONV_DIM
COL_LRU = COL_DT + SSM_HEADS
D_MAIN = COL_DT

NT_DIMS = (((1,), (1,)), ((), ()))


def _params(*sem):
    return pltpu.CompilerParams(dimension_semantics=sem, vmem_limit_bytes=VMEM_LIMIT)


def _sigmoid(x):
    return 1.0 / (1.0 + jnp.exp(-x))


def _silu(x):
    return x * _sigmoid(x)


def _softplus(x):
    return jnp.maximum(x, 0.0) + jnp.log1p(jnp.exp(-jnp.abs(x)))


def _rmsnorm_kernel(x_ref, w_ref, o_ref):
    x = x_ref[...]
    ms = jnp.mean(x * x, axis=-1, keepdims=True)
    o_ref[...] = (x * lax.rsqrt(ms + NORM_EPS) * w_ref[...]).astype(o_ref.dtype)


def _rmsnorm(x, w, out_dtype):
    t, d = x.shape
    tm = min(512, t)
    return pl.pallas_call(
        _rmsnorm_kernel,
        grid=(t // tm,),
        in_specs=[pl.BlockSpec((tm, d), lambda i: (i, 0)),
                  pl.BlockSpec((1, d), lambda i: (0, 0))],
        out_specs=pl.BlockSpec((tm, d), lambda i: (i, 0)),
        out_shape=jax.ShapeDtypeStruct((t, d), out_dtype),
        compiler_params=_params("parallel"),
        name="rmsnorm",
    )(x, w.reshape(1, d))


def _inproj_kernel(a_ref, w_ref, o_ref, wb_ref):
    @pl.when(pl.program_id(1) == 0)
    def _():
        wb_ref[...] = w_ref[...].astype(BF16)

    o_ref[...] = jnp.dot(a_ref[...], wb_ref[...], preferred_element_type=F32).astype(o_ref.dtype)


def _inproj(h, w, layer, n):
    t, k = h.shape
    tm = min(512, t)
    tn = 1024
    return pl.pallas_call(
        _inproj_kernel,
        grid=(n // tn, t // tm),
        in_specs=[pl.BlockSpec((tm, k), lambda j, i: (i, 0)),
                  pl.BlockSpec((None, k, tn), lambda j, i: (layer, 0, j))],
        out_specs=pl.BlockSpec((tm, tn), lambda j, i: (i, j)),
        out_shape=jax.ShapeDtypeStruct((t, n), BF16),
        scratch_shapes=[pltpu.VMEM((k, tn), BF16)],
        compiler_params=_params("parallel", "arbitrary"),
        name="inproj",
    )(h, w)


def _inproj_shifted_kernel(a_ref, w_ref, wnext_ref, o_ref, wb_ref, *, shift):
    @pl.when(pl.program_id(1) == 0)
    def _():
        k, tn = wb_ref.shape
        rows = 256
        for r in range(0, k, rows):
            win = jnp.concatenate([w_ref[r:r + rows, :], wnext_ref[r:r + rows, :]], axis=1)
            wb_ref[r:r + rows, :] = win[:, shift:shift + tn].astype(BF16)

    o_ref[...] = jnp.dot(a_ref[...], wb_ref[...], preferred_element_type=F32).astype(o_ref.dtype)


def _inproj_shifted(h, w, layer, col0, n):
    t, k = h.shape
    tm = min(1024, t)
    tn = 512
    shift = col0 % LANES
    base = col0 - shift
    assert base % tn == 0 and n % tn == 0 and 0 < shift
    return pl.pallas_call(
        functools.partial(_inproj_shifted_kernel, shift=shift),
        grid=(n // tn, t // tm),
        in_specs=[pl.BlockSpec((tm, k), lambda j, i: (i, 0)),
                  pl.BlockSpec((None, k, tn), lambda j, i: (layer, 0, base // tn + j)),
                  pl.BlockSpec((None, k, LANES), lambda j, i: (layer, 0, (base + tn * (j + 1)) // LANES))],
        out_specs=pl.BlockSpec((tm, tn), lambda j, i: (i, j)),
        out_shape=jax.ShapeDtypeStruct((t, n), BF16),
        scratch_shapes=[pltpu.VMEM((k, tn), BF16)],
        compiler_params=_params("parallel", "arbitrary"),
        name="inproj_shifted",
    )(h, w, w)


def _retention_kernel(q_ref, k_ref, v_ref, g_ref, gn_ref, o_ref, state_ref):
    c = pl.program_id(1)

    @pl.when(c == 0)
    def _():
        state_ref[...] = jnp.zeros_like(state_ref)

    C, dh = CHUNK, RET_HEAD_DIM
    row = lax.broadcasted_iota(jnp.int32, (C, C), 0)
    col = lax.broadcasted_iota(jnp.int32, (C, C), 1)
    diff = (row - col).astype(F32)
    rowf = row.astype(F32)
    causal = row >= col
    for h in range(RET_HEADS):
        log_g = math.log1p(-(2.0 ** (-5.0 - h)))
        sl = slice(h * dh, (h + 1) * dh)
        dmat = jnp.where(causal, jnp.exp(log_g * jnp.where(causal, diff, 0.0)), 0.0)
        q_decay = jnp.exp(log_g * (rowf + 1.0))
        k_decay = jnp.exp(log_g * (C - 1.0 - rowf))
        qb = q_ref[:, sl]
        k = k_ref[:, sl].astype(F32) * (dh ** -0.5)
        kb = k.astype(BF16)
        vb = v_ref[:, sl]
        state = state_ref[h]
        inner = lax.dot_general(qb, kb, NT_DIMS, preferred_element_type=F32) * dmat
        y = jnp.dot(inner.astype(BF16), vb, preferred_element_type=F32)
        y = y + jnp.dot(qb, state.astype(BF16), preferred_element_type=F32) * q_decay
        kdT = (k * k_decay).T.astype(BF16)
        state_ref[h] = math.exp(log_g * C) * state + jnp.dot(kdT, vb, preferred_element_type=F32)
        ms = jnp.mean(y * y, axis=-1, keepdims=True)
        y = y * lax.rsqrt(ms + NORM_EPS) * gn_ref[h]
        o_ref[:, sl] = (y * _silu(g_ref[:, sl].astype(F32))).astype(o_ref.dtype)


def _retention(proj, gn_w, batch, seq):
    nc = seq // CHUNK
    t = batch * seq
    base = COL_RET // D_BRANCH

    def spec(j):
        return pl.BlockSpec((CHUNK, D_BRANCH), lambda b, c, j=j: (b * nc + c, base + j))

    return pl.pallas_call(
        _retention_kernel,
        grid=(batch, nc),
        in_specs=[spec(0), spec(1), spec(2), spec(3),
                  pl.BlockSpec((RET_HEADS, 1, RET_HEAD_DIM), lambda b, c: (0, 0, 0))],
        out_specs=pl.BlockSpec((CHUNK, D_BRANCH), lambda b, c: (b * nc + c, 0)),
        out_shape=jax.ShapeDtypeStruct((t, D_BRANCH), BF16),
        scratch_shapes=[pltpu.VMEM((RET_HEADS, RET_HEAD_DIM, RET_HEAD_DIM), F32)],
        compiler_params=_params("parallel", "arbitrary"),
        name="retention",
    )(proj, proj, proj, proj, gn_w.reshape(RET_HEADS, 1, RET_HEAD_DIM))


BF16_TERMS = 3
MOBA_VT_ROWS = MOBA_HEAD_DIM + 16
MOBA_M_INIT = -1e30
LOG2E = math.log2(math.e)


def _split_bf16(x):
    terms = []
    for _ in range(BF16_TERMS - 1):
        hi = x.astype(BF16).astype(F32)
        terms.append(hi)
        x = x - hi
    return terms + [x]


def _moba_kernel(slopes_ref, q_ref, k_ref, v_ref, g_ref, o_ref, kmean_ref, kx_ref, vt_ref, sel_ref):
    hg = pl.program_id(1)
    i = pl.program_id(2)
    L, dh, HP = CHUNK, MOBA_HEAD_DIM, MOBA_HEADS_PER_STEP
    seq = k_ref.shape[0]
    nb = seq // L
    nbp = kmean_ref.shape[1]
    neg = -jnp.inf
    lane = lax.broadcasted_iota(jnp.int32, (L, LANES), 1)

    @pl.when(i == 0)
    def _():
        kmean_ref[...] = jnp.zeros_like(kmean_ref)
        vt_ref[:, :, dh:, :] = jnp.ones((HP, nb, MOBA_VT_ROWS - dh, L), BF16)
        within = lax.broadcasted_iota(jnp.int32, (L, LANES), 0).astype(F32)
        for n in range(nb):
            kx = jnp.where(lane < BF16_TERMS, within,
                           jnp.where(lane < 2 * BF16_TERMS, float(n), 0.0))
            kx_ref[n * L:(n + 1) * L, :] = kx.astype(BF16)
            for hh in range(HP):
                hs = slice(hh * dh, (hh + 1) * dh)
                kn = k_ref[n * L:(n + 1) * L, hs].astype(F32)
                kmean_ref[hh, n:n + 1, :] = jnp.mean(kn, axis=0, keepdims=True)
                vn = v_ref[n * L:(n + 1) * L, hs].astype(F32)
                vt_ref[hh, n, 0:dh, :] = vn.T.astype(BF16)

    blk_row = lax.broadcasted_iota(jnp.int32, (nbp, L), 0)
    blk_rowf = blk_row.astype(F32)
    key = lax.broadcasted_iota(jnp.int32, (L, L), 0)
    qry = lax.broadcasted_iota(jnp.int32, (L, L), 1)
    lane1 = lax.broadcasted_iota(jnp.int32, (1, LANES), 1)

    def scores(hh, qa, off):
        hs = slice(hh * dh, (hh + 1) * dh)
        ka = jnp.concatenate([k_ref[pl.ds(off, L), hs], kx_ref[pl.ds(off, L), :]], axis=1)
        return lax.dot_general(ka, qa, NT_DIMS, preferred_element_type=F32)

    qas, carry0 = [], []
    for hh in range(HP):
        hs = slice(hh * dh, (hh + 1) * dh)
        q = q_ref[:, hs].astype(F32)
        gate = lax.dot_general(kmean_ref[hh], q, NT_DIMS, precision=HIGHEST,
                               preferred_element_type=F32)
        gm = jnp.where(blk_row < i, gate, neg)
        sel = jnp.zeros(gate.shape, F32)
        for _ in range(MOBA_TOPK):
            mx = jnp.max(gm, axis=0, keepdims=True)
            first = jnp.min(jnp.where(gm == mx, blk_rowf, float(nbp)), axis=0, keepdims=True)
            pick = jnp.logical_and(blk_rowf == first, mx > neg)
            sel = jnp.where(pick, 1.0, sel)
            gm = jnp.where(pick, neg, gm)
        sel_ref[hh] = sel
        slope2 = jnp.full((1, LANES), slopes_ref[hg * HP + hh] * LOG2E, F32)
        qx = jnp.zeros((1, LANES), F32)
        for t, term in enumerate(_split_bf16(slope2) + _split_bf16(slope2 * L)):
            qx = jnp.where(lane1 == t, term, qx)
        qa = jnp.concatenate([(q * (dh ** -0.5 * LOG2E)).astype(BF16),
                              jnp.broadcast_to(qx, (L, LANES)).astype(BF16)], axis=1)
        qas.append(qa)
        carry0.append((jnp.full((1, L), MOBA_M_INIT, F32), jnp.zeros((MOBA_VT_ROWS, L), F32)))

    def update(hh, blk, s, m, acc, picked):
        cmax = jnp.max(s, axis=0, keepdims=True)
        if picked is None:
            m_new = jnp.maximum(m, cmax)
            p = jnp.exp2(s - m_new)
        else:
            m_new = jnp.maximum(m, jnp.where(picked, cmax, neg))
            p = jnp.exp2(s - jnp.where(picked, m_new, jnp.inf))
        acc = acc * jnp.exp2(m - m_new) + jnp.dot(vt_ref[hh, blk], p.astype(BF16),
                                                  preferred_element_type=F32)
        return m_new, acc

    def body(j, carry):
        s_cur, stats = carry
        nxt = pl.multiple_of((j + 1) * L, L)
        s_next = tuple(scores(hh, qas[hh], nxt) for hh in range(HP))
        new = []
        for hh in range(HP):
            picked = sel_ref[hh, pl.ds(j, 1), :] > 0.0
            new.append(update(hh, j, s_cur[hh], *stats[hh], picked))
        return s_next, tuple(new)

    s_first = tuple(scores(hh, qas[hh], 0) for hh in range(HP))
    s_own, final = lax.fori_loop(0, i, body, (s_first, tuple(carry0)))
    for hh in range(HP):
        hs = slice(hh * dh, (hh + 1) * dh)
        _, acc = update(hh, i, jnp.where(key <= qry, s_own[hh], neg), *final[hh], None)
        out = (acc[:dh, :] / acc[dh:dh + 1, :]).T
        o_ref[:, hs] = (out * _silu(g_ref[:, hs].astype(F32))).astype(o_ref.dtype)


def _moba(proj, batch, seq):
    nb = seq // CHUNK
    t = batch * seq
    H, dh, HP = MOBA_HEADS, MOBA_HEAD_DIM, MOBA_HEADS_PER_STEP
    nbp = -(-nb // SUBLANES) * SUBLANES
    w = HP * dh
    base = COL_MOBA // w
    per = D_BRANCH // w
    slopes = jnp.asarray(np.exp2(-8.0 * (np.arange(H, dtype=np.float64) + 1.0) / H), F32)
    once = pl.Buffered(1)
    return pl.pallas_call(
        _moba_kernel,
        grid=(batch, H // HP, nb),
        in_specs=[pl.BlockSpec(memory_space=pltpu.SMEM),
                  pl.BlockSpec((CHUNK, w), lambda b, h, i: (b * nb + i, base + h)),
                  pl.BlockSpec((seq, w), lambda b, h, i: (b, base + per + h), pipeline_mode=once),
                  pl.BlockSpec((seq, w), lambda b, h, i: (b, base + 2 * per + h), pipeline_mode=once),
                  pl.BlockSpec((CHUNK, w), lambda b, h, i: (b * nb + i, base + 3 * per + h))],
        out_specs=pl.BlockSpec((CHUNK, w), lambda b, h, i: (b * nb + i, h)),
        out_shape=jax.ShapeDtypeStruct((t, D_BRANCH), BF16),
        scratch_shapes=[pltpu.VMEM((HP, nbp, dh), F32),
                        pltpu.VMEM((seq, LANES), BF16),
                        pltpu.VMEM((HP, nb, MOBA_VT_ROWS, CHUNK), BF16),
                        pltpu.VMEM((HP, nbp, CHUNK), F32)],
        compiler_params=_params("parallel", "parallel", "arbitrary"),
        name="moba",
    )(slopes, proj, proj, proj, proj)


def _causal_conv(buf_ref, parts, cw_ref, cb_ref, first):
    n = parts[0].shape[0]

    @pl.when(first)
    def _():
        buf_ref[0:SUBLANES, :] = jnp.zeros((SUBLANES, buf_ref.shape[1]), F32)

    c0 = 0
    for part in parts:
        buf_ref[SUBLANES:SUBLANES + n, c0:c0 + part.shape[1]] = part.astype(F32)
        c0 += part.shape[1]
    y = cb_ref[...]
    for k in range(CONV_WIDTH):
        off = SUBLANES - (CONV_WIDTH - 1) + k
        y = y + buf_ref[off:off + n, :] * cw_ref[k:k + 1, :]
    buf_ref[0:SUBLANES, :] = buf_ref[n:n + SUBLANES, :]
    return y


def _ssd_kernel(z_ref, xs_ref, bc_ref, h_ref, wdt_ref, wdtT_ref, cw_ref, cb_ref, dtb_ref, dtbT_ref,
                alog_ref, alogT_ref, dexp_ref, nw_ref, e_ref, o_ref, buf_ref, state_ref):
    c = pl.program_id(1)
    Q, N, G = CHUNK, SSM_STATE, SSM_GROUPS
    R = SSM_HEADS // G
    GW = D_BRANCH // G

    @pl.when(c == 0)
    def _():
        state_ref[...] = jnp.zeros_like(state_ref)

    act = _silu(_causal_conv(buf_ref, [xs_ref[...], bc_ref[...]], cw_ref, cb_ref, c == 0))
    xs = act[:, :D_BRANCH]
    bm = act[:, D_BRANCH:D_BRANCH + G * N]
    cm = act[:, D_BRANCH + G * N:]

    hb = h_ref[...]
    dt = _softplus(jnp.dot(hb, wdt_ref[...], preferred_element_type=F32) + dtb_ref[...])
    dtT = _softplus(lax.dot_general(wdtT_ref[...], hb, NT_DIMS, preferred_element_type=F32)
                    + dtbT_ref[...])
    da = dt * (-jnp.exp(alog_ref[...]))
    daT = dtT * (-jnp.exp(alogT_ref[...]))

    row = lax.broadcasted_iota(jnp.int32, (Q, Q), 0)
    col = lax.broadcasted_iota(jnp.int32, (Q, Q), 1)
    tri = row >= col
    lower = jnp.where(tri, 1.0, 0.0).astype(BF16)
    upper = jnp.where(row <= col, 1.0, 0.0).astype(BF16)
    heads = daT.shape[0]
    cs3 = jnp.dot(lower, jnp.concatenate([p.astype(BF16) for p in _split_bf16(da)], axis=1),
                  preferred_element_type=F32)
    da_cs = cs3[:, :LANES] + cs3[:, LANES:2 * LANES] + cs3[:, 2 * LANES:]
    csT3 = jnp.dot(jnp.concatenate([p.astype(BF16) for p in _split_bf16(daT)], axis=0), upper,
                   preferred_element_type=F32)
    da_csT = csT3[:heads] + csT3[heads:2 * heads] + csT3[2 * heads:]

    def expand(v):
        lhs = jnp.concatenate([p.astype(BF16) for p in _split_bf16(v)], axis=1)
        return jnp.dot(lhs, e_ref[...], preferred_element_type=F32)

    dt_exp = expand(dt)
    cs_exp = expand(da_cs)
    last_exp = cs_exp[Q - 1:Q, :]
    xdt = xs * dt_exp
    x_w = xdt * jnp.exp(last_exp - cs_exp)
    chunk_decay = jnp.exp(last_exp)
    in_decay = jnp.exp(cs_exp)

    lane = lax.broadcasted_iota(jnp.int32, (Q, GW), 1)
    for g in range(G):
        gs = slice(g * GW, (g + 1) * GW)
        bg = bm[:, g * N:(g + 1) * N]
        cgb = cm[:, g * N:(g + 1) * N].astype(BF16)
        bgT = bg.T.astype(BF16)
        cb = jnp.dot(cgb, bgT, preferred_element_type=F32)
        st = state_ref[:, gs]
        y_off = jnp.dot(cgb, st.astype(BF16), preferred_element_type=F32)
        state_ref[:, gs] = st * chunk_decay[:, gs] + jnp.dot(
            bgT, x_w[:, gs].astype(BF16), preferred_element_type=F32)
        xdt_g = xdt[:, gs]
        w_parts, x_parts = [], []
        for r in range(R):
            hd = g * R + r
            seg = da_cs[:, hd:hd + 1] - da_csT[hd:hd + 1, :]
            lmat = jnp.exp(jnp.where(tri, seg, -jnp.inf))
            w_parts.append((cb * lmat).astype(BF16))
            in_head = jnp.logical_and(lane >= r * SSM_HEAD_DIM, lane < (r + 1) * SSM_HEAD_DIM)
            x_parts.append(jnp.where(in_head, xdt_g, 0.0).astype(BF16))
        y_diag = jnp.dot(jnp.concatenate(w_parts, axis=1), jnp.concatenate(x_parts, axis=0),
                         preferred_element_type=F32)
        y = y_diag + y_off * in_decay[:, gs] + dexp_ref[:, gs] * xs[:, gs]
        y = y * _silu(z_ref[:, gs].astype(F32))
        ms = jnp.mean(y * y, axis=-1, keepdims=True)
        o_ref[:, gs] = (y * lax.rsqrt(ms + NORM_EPS) * nw_ref[:, gs]).astype(o_ref.dtype)


def _ssd(proj, hnorm, w_dt, conv_w, conv_b, dt_bias, a_log, d_skip, norm_w, batch, seq):
    nc = seq // CHUNK
    t = batch * seq
    d = hnorm.shape[1]
    H = SSM_HEADS
    w_dt_pad = jnp.pad(w_dt, ((0, 0), (0, LANES - H))).astype(BF16)
    w_dtT = w_dt.T.astype(BF16)
    pad = lambda v: jnp.pad(v.reshape(1, H), ((0, 0), (0, LANES - H)))
    expand = np.zeros((LANES, D_BRANCH), np.float32)
    for hd in range(H):
        expand[hd, hd * SSM_HEAD_DIM:(hd + 1) * SSM_HEAD_DIM] = 1.0
    d_exp = jnp.repeat(d_skip, SSM_HEAD_DIM).reshape(1, D_BRANCH)
    const = lambda shape: pl.BlockSpec(shape, lambda b, c: (0,) * len(shape))
    blk = lambda col0: pl.BlockSpec((CHUNK, D_BRANCH), lambda b, c: (b * nc + c, col0 // D_BRANCH))
    return pl.pallas_call(
        _ssd_kernel,
        grid=(batch, nc),
        in_specs=[blk(COL_Z), blk(COL_XS), blk(COL_BC),
                  pl.BlockSpec((CHUNK, d), lambda b, c: (b * nc + c, 0)),
                  const((d, LANES)), const((H, d)),
                  const((CONV_WIDTH, SSM_CONV_DIM)), const((1, SSM_CONV_DIM)),
                  const((1, LANES)), const((H, 1)), const((1, LANES)), const((H, 1)),
                  const((1, D_BRANCH)), const((1, D_BRANCH)), const((BF16_TERMS * LANES, D_BRANCH))],
        out_specs=pl.BlockSpec((CHUNK, D_BRANCH), lambda b, c: (b * nc + c, 0)),
        out_shape=jax.ShapeDtypeStruct((t, D_BRANCH), BF16),
        scratch_shapes=[pltpu.VMEM((CHUNK + 2 * SUBLANES, SSM_CONV_DIM), F32),
                        pltpu.VMEM((SSM_STATE, D_BRANCH), F32)],
        compiler_params=_params("parallel", "arbitrary"),
        name="ssd",
    )(proj, proj, proj, hnorm, w_dt_pad, w_dtT, conv_w, conv_b.reshape(1, -1),
      pad(dt_bias), dt_bias.reshape(H, 1), pad(a_log), a_log.reshape(H, 1),
      d_exp, norm_w.reshape(1, D_BRANCH), jnp.asarray(np.tile(expand, (BF16_TERMS, 1)), BF16))


def _lru_kernel(x_ref, g_ref, cw_ref, cb_ref, wa_ref, ba_ref, wx_ref, bx_ref, lam_ref,
                o_ref, buf_ref, carry_ref):
    c = pl.program_id(1)
    n = x_ref.shape[0]

    @pl.when(c == 0)
    def _():
        carry_ref[...] = jnp.zeros_like(carry_ref)

    xc = _causal_conv(buf_ref, [x_ref[...]], cw_ref, cb_ref, c == 0)
    ra, ix = [], []
    for blk in range(LRU_BLOCKS):
        xb = xc[:, blk * LRU_BLOCK_DIM:(blk + 1) * LRU_BLOCK_DIM].astype(BF16)
        ra.append(jnp.dot(xb, wa_ref[blk], preferred_element_type=F32))
        ix.append(jnp.dot(xb, wx_ref[blk], preferred_element_type=F32))
    r = _sigmoid(jnp.concatenate(ra, axis=1) + ba_ref[...])
    i = _sigmoid(jnp.concatenate(ix, axis=1) + bx_ref[...])
    log_a = -LRU_C * r * _softplus(-lam_ref[...])
    a = jnp.exp(log_a)
    u = jnp.sqrt(-jnp.tanh(log_a) * (1.0 + a * a)) * (i * xc)

    sub = jnp.bitwise_and(lax.broadcasted_iota(jnp.int32, a.shape, 0), SUBLANES - 1)
    d = 1
    while d < SUBLANES:
        keep = sub >= d
        a_sh = jnp.where(keep, pltpu.roll(a, d, 0), 1.0)
        u_sh = jnp.where(keep, pltpu.roll(u, d, 0), 0.0)
        u = a * u_sh + u
        a = a * a_sh
        d *= 2
    h = carry_ref[...]
    groups = []
    for r in range(0, n, SUBLANES):
        groups.append(u[r:r + SUBLANES, :] + a[r:r + SUBLANES, :] * h)
        h = groups[-1][SUBLANES - 1:SUBLANES, :]
    carry_ref[...] = h
    hseq = jnp.concatenate(groups, axis=0)
    o_ref[...] = (hseq * _silu(g_ref[...].astype(F32))).astype(o_ref.dtype)


def _lru(proj, conv_w, conv_b, w_a, b_a, w_x, b_x, lam, batch, seq):
    lc = CHUNK
    nc = seq // lc
    t = batch * seq
    const = lambda shape: pl.BlockSpec(shape, lambda b, c: (0,) * len(shape))
    wshape = (LRU_BLOCKS, LRU_BLOCK_DIM, LRU_BLOCK_DIM)
    return pl.pallas_call(
        _lru_kernel,
        grid=(batch, nc),
        in_specs=[pl.BlockSpec((lc, D_BRANCH), lambda b, c: (b * nc + c, 0)),
                  pl.BlockSpec((lc, D_BRANCH), lambda b, c: (b * nc + c, 1)),
                  const((CONV_WIDTH, D_BRANCH)), const((1, D_BRANCH)),
                  const(wshape), const((1, D_BRANCH)), const(wshape), const((1, D_BRANCH)),
                  const((1, D_BRANCH))],
        out_specs=pl.BlockSpec((lc, D_BRANCH), lambda b, c: (b * nc + c, 0)),
        out_shape=jax.ShapeDtypeStruct((t, D_BRANCH), BF16),
        scratch_shapes=[pltpu.VMEM((lc + 2 * SUBLANES, D_BRANCH), F32),
                        pltpu.VMEM((1, D_BRANCH), F32)],
        compiler_params=_params("parallel", "arbitrary"),
        name="rglru",
    )(proj, proj, conv_w, conv_b.reshape(1, -1), w_a.astype(BF16), b_a.reshape(1, -1),
      w_x.astype(BF16), b_x.reshape(1, -1), lam.reshape(1, -1))


def _outproj_kernel(y0_ref, y1_ref, y2_ref, y3_ref, w_ref, x_ref, o_ref, wb_ref):
    @pl.when(pl.program_id(1) == 0)
    def _():
        wb_ref[...] = w_ref[...].astype(BF16)

    acc = x_ref[...]
    for j, y_ref in enumerate((y0_ref, y1_ref, y2_ref, y3_ref)):
        acc = acc + jnp.dot(y_ref[...], wb_ref[j * D_BRANCH:(j + 1) * D_BRANCH, :],
                            preferred_element_type=F32)
    o_ref[...] = acc


def _outproj(ys, w, layer, x):
    t, d = x.shape
    k = w.shape[1]
    tm = min(1024, t)
    tn = 512
    yspec = pl.BlockSpec((tm, D_BRANCH), lambda j, i: (i, 0))
    return pl.pallas_call(
        _outproj_kernel,
        grid=(d // tn, t // tm),
        in_specs=[yspec, yspec, yspec, yspec,
                  pl.BlockSpec((None, k, tn), lambda j, i: (layer, 0, j)),
                  pl.BlockSpec((tm, tn), lambda j, i: (i, j))],
        out_specs=pl.BlockSpec((tm, tn), lambda j, i: (i, j)),
        out_shape=jax.ShapeDtypeStruct((t, d), F32),
        scratch_shapes=[pltpu.VMEM((k, tn), BF16)],
        compiler_params=_params("parallel", "arbitrary"),
        name="outproj",
    )(*ys, w, x)


def kernel(x, norm_w, w_in, ret_gn_w, ssm_conv_w, ssm_conv_b, ssm_dt_bias, ssm_a_log, ssm_d, ssm_norm_w, lru_conv_w, lru_conv_b, lru_w_a, lru_b_a, lru_w_x, lru_b_x, lru_lambda, w_out, final_norm_w):
    batch, seq, d = x.shape
    depth = w_in.shape[0]
    assert seq % CHUNK == 0 and d == 4 * D_BRANCH
    t = batch * seq
    xf = x.reshape(t, d)
    for l in range(depth):
        hn = _rmsnorm(xf, norm_w[l], BF16)
        proj = _inproj(hn, w_in, l, D_MAIN)
        proj_lru = _inproj_shifted(hn, w_in, l, COL_LRU, 2 * D_BRANCH)
        y_ret = _retention(proj, ret_gn_w[l], batch, seq)
        y_moba = _moba(proj, batch, seq)
        y_ssm = _ssd(proj, hn, w_in[l, :, COL_DT:COL_LRU], ssm_conv_w[l], ssm_conv_b[l],
                     ssm_dt_bias[l], ssm_a_log[l], ssm_d[l], ssm_norm_w[l], batch, seq)
        y_lru = _lru(proj_lru, lru_conv_w[l], lru_conv_b[l], lru_w_a[l], lru_b_a[l], lru_w_x[l],
                     lru_b_x[l], lru_lambda[l], batch, seq)
        xf = _outproj((y_ret, y_moba, y_ssm, y_lru), w_out, l, xf)
    return _rmsnorm(xf, final_norm_w, F32).reshape(batch, seq, d)
```

```python
import functools
import math

import jax
import jax.numpy as jnp
import numpy as np
from jax import lax
from jax.experimental import pallas as pl
from jax.experimental.pallas import tpu as pltpu

F32 = jnp.float32
BF16 = jnp.bfloat16
HIGHEST = lax.Precision.HIGHEST

NORM_EPS = 1e-6
D_BRANCH = 1024
CHUNK = 256
RET_HEADS = 4
RET_HEAD_DIM = 256
MOBA_HEADS = 8
MOBA_HEAD_DIM = 128
MOBA_TOPK = 3
MOBA_HEADS_PER_STEP = 8
SSM_HEADS = 16
SSM_HEAD_DIM = 64
SSM_GROUPS = 4
SSM_STATE = 128
SSM_CONV_DIM = 2048
CONV_WIDTH = 4
LRU_BLOCKS = 8
LRU_BLOCK_DIM = 128
LRU_C = 8.0
LANES = 128
SUBLANES = 8
VMEM_LIMIT = 56 * 1024 * 1024

COL_RET = 0
COL_MOBA = 4 * D_BRANCH
COL_Z = 8 * D_BRANCH
COL_XS = COL_Z + D_BRANCH
COL_BC = COL_XS + D_BRANCH
COL_DT = COL_XS + SSM_CONV_DIM
COL_LRU = COL_DT + SSM_HEADS
D_MAIN = COL_DT

NT_DIMS = (((1,), (1,)), ((), ()))


def _params(*sem):
    return pltpu.CompilerParams(dimension_semantics=sem, vmem_limit_bytes=VMEM_LIMIT)


def _sigmoid(x):
    return 1.0 / (1.0 + jnp.exp(-x))


def _silu(x):
    return x * _sigmoid(x)


def _softplus(x):
    return jnp.maximum(x, 0.0) + jnp.log1p(jnp.exp(-jnp.abs(x)))


def _rmsnorm_kernel(x_ref, w_ref, o_ref):
    x = x_ref[...]
    ms = jnp.mean(x * x, axis=-1, keepdims=True)
    o_ref[...] = (x * lax.rsqrt(ms + NORM_EPS) * w_ref[...]).astype(o_ref.dtype)


def _rmsnorm(x, w, out_dtype):
    t, d = x.shape
    tm = min(512, t)
    return pl.pallas_call(
        _rmsnorm_kernel,
        grid=(t // tm,),
        in_specs=[pl.BlockSpec((tm, d), lambda i: (i, 0)),
                  pl.BlockSpec((1, d), lambda i: (0, 0))],
        out_specs=pl.BlockSpec((tm, d), lambda i: (i, 0)),
        out_shape=jax.ShapeDtypeStruct((t, d), out_dtype),
        compiler_params=_params("parallel"),
        name="rmsnorm",
    )(x, w.reshape(1, d))


def _inproj_kernel(a_ref, wt_ref, o_ref, wb_ref):
    @pl.when(pl.program_id(1) == 0)
    def _():
        wb_ref[...] = wt_ref[...].astype(BF16)

    o_ref[...] = lax.dot_general(a_ref[...], wb_ref[...], NT_DIMS,
                                 preferred_element_type=F32).astype(o_ref.dtype)


def _inproj(h, wt, layer, row0, n):
    t, k = h.shape
    tm = min(512, t)
    tn = 1024
    assert n % tn == 0 and row0 % SUBLANES == 0
    if row0 % tn == 0:
        wspec = pl.BlockSpec((None, tn, k), lambda j, i: (layer, row0 // tn + j, 0))
    else:
        wspec = pl.BlockSpec((None, pl.Element(tn), pl.Element(k)),
                             lambda j, i: (layer, pl.multiple_of(row0 + tn * j, SUBLANES), 0))
    return pl.pallas_call(
        _inproj_kernel,
        grid=(n // tn, t // tm),
        in_specs=[pl.BlockSpec((tm, k), lambda j, i: (i, 0)), wspec],
        out_specs=pl.BlockSpec((tm, tn), lambda j, i: (i, j)),
        out_shape=jax.ShapeDtypeStruct((t, n), BF16),
        scratch_shapes=[pltpu.VMEM((tn, k), BF16)],
        compiler_params=_params("parallel", "arbitrary"),
        name="inproj",
    )(h, wt)


def _retention_kernel(q_ref, k_ref, v_ref, g_ref, gn_ref, o_ref, state_ref):
    c = pl.program_id(1)

    @pl.when(c == 0)
    def _():
        state_ref[...] = jnp.zeros_like(state_ref)

    C, dh = CHUNK, RET_HEAD_DIM
    row = lax.broadcasted_iota(jnp.int32, (C, C), 0)
    col = lax.broadcasted_iota(jnp.int32, (C, C), 1)
    diff = (row - col).astype(F32)
    rowf = row.astype(F32)
    causal = row >= col
    for h in range(RET_HEADS):
        log_g = math.log1p(-(2.0 ** (-5.0 - h)))
        sl = slice(h * dh, (h + 1) * dh)
        dmat = jnp.where(causal, jnp.exp(log_g * jnp.where(causal, diff, 0.0)), 0.0)
        q_decay = jnp.exp(log_g * (rowf + 1.0))
        k_decay = jnp.exp(log_g * (C - 1.0 - rowf))
        qb = q_ref[:, sl]
        k = k_ref[:, sl].astype(F32) * (dh ** -0.5)
        kb = k.astype(BF16)
        vb = v_ref[:, sl]
        state = state_ref[h]
        inner = lax.dot_general(qb, kb, NT_DIMS, preferred_element_type=F32) * dmat
        y = jnp.dot(inner.astype(BF16), vb, preferred_element_type=F32)
        y = y + jnp.dot(qb, state.astype(BF16), preferred_element_type=F32) * q_decay
        kdT = (k * k_decay).T.astype(BF16)
        state_ref[h] = math.exp(log_g * C) * state + jnp.dot(kdT, vb, preferred_element_type=F32)
        ms = jnp.mean(y * y, axis=-1, keepdims=True)
        y = y * lax.rsqrt(ms + NORM_EPS) * gn_ref[h]
        o_ref[:, sl] = (y * _silu(g_ref[:, sl].astype(F32))).astype(o_ref.dtype)


def _retention(proj, gn_w, batch, seq):
    nc = seq // CHUNK
    t = batch * seq
    base = COL_RET // D_BRANCH

    def spec(j):
        return pl.BlockSpec((CHUNK, D_BRANCH), lambda b, c, j=j: (b * nc + c, base + j))

    return pl.pallas_call(
        _retention_kernel,
        grid=(batch, nc),
        in_specs=[spec(0), spec(1), spec(2), spec(3),
                  pl.BlockSpec((RET_HEADS, 1, RET_HEAD_DIM), lambda b, c: (0, 0, 0))],
        out_specs=pl.BlockSpec((CHUNK, D_BRANCH), lambda b, c: (b * nc + c, 0)),
        out_shape=jax.ShapeDtypeStruct((t, D_BRANCH), BF16),
        scratch_shapes=[pltpu.VMEM((RET_HEADS, RET_HEAD_DIM, RET_HEAD_DIM), F32)],
        compiler_params=_params("parallel", "arbitrary"),
        name="retention",
    )(proj, proj, proj, proj, gn_w.reshape(RET_HEADS, 1, RET_HEAD_DIM))


BF16_TERMS = 3
MOBA_VT_ROWS = MOBA_HEAD_DIM + 16
MOBA_M_INIT = -1e30
LOG2E = math.log2(math.e)


def _split_bf16(x):
    terms = []
    for _ in range(BF16_TERMS - 1):
        hi = x.astype(BF16).astype(F32)
        terms.append(hi)
        x = x - hi
    return terms + [x]


def _moba_kernel(slopes_ref, q_ref, k_ref, v_ref, g_ref, o_ref, kmean_ref, kx_ref, vt_ref, sel_ref):
    hg = pl.program_id(1)
    i = pl.program_id(2)
    L, dh, HP = CHUNK, MOBA_HEAD_DIM, MOBA_HEADS_PER_STEP
    seq = k_ref.shape[0]
    nb = seq // L
    nbp = kmean_ref.shape[1]
    neg = -jnp.inf
    lane = lax.broadcasted_iota(jnp.int32, (L, LANES), 1)

    @pl.when(i == 0)
    def _():
        kmean_ref[...] = jnp.zeros_like(kmean_ref)
        vt_ref[:, :, dh:, :] = jnp.ones((HP, nb, MOBA_VT_ROWS - dh, L), BF16)
        within = lax.broadcasted_iota(jnp.int32, (L, LANES), 0).astype(F32)
        for n in range(nb):
            kx = jnp.where(lane < BF16_TERMS, within,
                           jnp.where(lane < 2 * BF16_TERMS, float(n), 0.0))
            kx_ref[n * L:(n + 1) * L, :] = kx.astype(BF16)
            for hh in range(HP):
                hs = slice(hh * dh, (hh + 1) * dh)
                kn = k_ref[n * L:(n + 1) * L, hs].astype(F32)
                kmean_ref[hh, n:n + 1, :] = jnp.mean(kn, axis=0, keepdims=True)
                vn = v_ref[n * L:(n + 1) * L, hs].astype(F32)
                vt_ref[hh, n, 0:dh, :] = vn.T.astype(BF16)

    blk_row = lax.broadcasted_iota(jnp.int32, (nbp, L), 0)
    blk_rowf = blk_row.astype(F32)
    key = lax.broadcasted_iota(jnp.int32, (L, L), 0)
    qry = lax.broadcasted_iota(jnp.int32, (L, L), 1)
    lane1 = lax.broadcasted_iota(jnp.int32, (1, LANES), 1)

    def scores(hh, qa, off):
        hs = slice(hh * dh, (hh + 1) * dh)
        ka = jnp.concatenate([k_ref[pl.ds(off, L), hs], kx_ref[pl.ds(off, L), :]], axis=1)
        return lax.dot_general(ka, qa, NT_DIMS, preferred_element_type=F32)

    qas, carry0 = [], []
    for hh in range(HP):
        hs = slice(hh * dh, (hh + 1) * dh)
        q = q_ref[:, hs].astype(F32)
        gate = lax.dot_general(kmean_ref[hh], q, NT_DIMS, precision=HIGHEST,
                               preferred_element_type=F32)
        gm = jnp.where(blk_row < i, gate, neg)
        sel = jnp.zeros(gate.shape, F32)
        for _ in range(MOBA_TOPK):
            mx = jnp.max(gm, axis=0, keepdims=True)
            first = jnp.min(jnp.where(gm == mx, blk_rowf, float(nbp)), axis=0, keepdims=True)
            pick = jnp.logical_and(blk_rowf == first, mx > neg)
            sel = jnp.where(pick, 1.0, sel)
            gm = jnp.where(pick, neg, gm)
        sel_ref[hh] = sel
        slope2 = jnp.full((1, LANES), slopes_ref[hg * HP + hh] * LOG2E, F32)
        qx = jnp.zeros((1, LANES), F32)
        for t, term in enumerate(_split_bf16(slope2) + _split_bf16(slope2 * L)):
            qx = jnp.where(lane1 == t, term, qx)
        qa = jnp.concatenate([(q * (dh ** -0.5 * LOG2E)).astype(BF16),
                              jnp.broadcast_to(qx, (L, LANES)).astype(BF16)], axis=1)
        qas.append(qa)
        carry0.append((jnp.full((1, L), MOBA_M_INIT, F32), jnp.zeros((MOBA_VT_ROWS, L), F32)))

    def update(hh, blk, s, m, acc, picked):
        cmax = jnp.max(s, axis=0, keepdims=True)
        if picked is None:
            m_new = jnp.maximum(m, cmax)
            p = jnp.exp2(s - m_new)
        else:
            m_new = jnp.maximum(m, jnp.where(picked, cmax, neg))
            p = jnp.exp2(s - jnp.where(picked, m_new, jnp.inf))
        acc = acc * jnp.exp2(m - m_new) + jnp.dot(vt_ref[hh, blk], p.astype(BF16),
                                                  preferred_element_type=F32)
        return m_new, acc

    def body(j, carry):
        s_cur, stats = carry
        nxt = pl.multiple_of((j + 1) * L, L)
        s_next = tuple(scores(hh, qas[hh], nxt) for hh in range(HP))
        new = []
        for hh in range(HP):
            picked = sel_ref[hh, pl.ds(j, 1), :] > 0.0
            new.append(update(hh, j, s_cur[hh], *stats[hh], picked))
        return s_next, tuple(new)

    s_first = tuple(scores(hh, qas[hh], 0) for hh in range(HP))
    s_own, final = lax.fori_loop(0, i, body, (s_first, tuple(carry0)))
    for hh in range(HP):
        hs = slice(hh * dh, (hh + 1) * dh)
        _, acc = update(hh, i, jnp.where(key <= qry, s_own[hh], neg), *final[hh], None)
        out = (acc[:dh, :] / acc[dh:dh + 1, :]).T
        o_ref[:, hs] = (out * _silu(g_ref[:, hs].astype(F32))).astype(o_ref.dtype)


def _moba(proj, batch, seq):
    nb = seq // CHUNK
    t = batch * seq
    H, dh, HP = MOBA_HEADS, MOBA_HEAD_DIM, MOBA_HEADS_PER_STEP
    nbp = -(-nb // SUBLANES) * SUBLANES
    w = HP * dh
    base = COL_MOBA // w
    per = D_BRANCH // w
    slopes = jnp.asarray(np.exp2(-8.0 * (np.arange(H, dtype=np.float64) + 1.0) / H), F32)
    once = pl.Buffered(1)
    return pl.pallas_call(
        _moba_kernel,
        grid=(batch, H // HP, nb),
        in_specs=[pl.BlockSpec(memory_space=pltpu.SMEM),
                  pl.BlockSpec((CHUNK, w), lambda b, h, i: (b * nb + i, base + h)),
                  pl.BlockSpec((seq, w), lambda b, h, i: (b, base + per + h), pipeline_mode=once),
                  pl.BlockSpec((seq, w), lambda b, h, i: (b, base + 2 * per + h), pipeline_mode=once),
                  pl.BlockSpec((CHUNK, w), lambda b, h, i: (b * nb + i, base + 3 * per + h))],
        out_specs=pl.BlockSpec((CHUNK, w), lambda b, h, i: (b * nb + i, h)),
        out_shape=jax.ShapeDtypeStruct((t, D_BRANCH), BF16),
        scratch_shapes=[pltpu.VMEM((HP, nbp, dh), F32),
                        pltpu.VMEM((seq, LANES), BF16),
                        pltpu.VMEM((HP, nb, MOBA_VT_ROWS, CHUNK), BF16),
                        pltpu.VMEM((HP, nbp, CHUNK), F32)],
        compiler_params=_params("parallel", "parallel", "arbitrary"),
        name="moba",
    )(slopes, proj, proj, proj, proj)


def _causal_conv(buf_ref, parts, cw_ref, cb_ref, first):
    n = parts[0].shape[0]

    @pl.when(first)
    def _():
        buf_ref[0:SUBLANES, :] = jnp.zeros((SUBLANES, buf_ref.shape[1]), F32)

    c0 = 0
    for part in parts:
        buf_ref[SUBLANES:SUBLANES + n, c0:c0 + part.shape[1]] = part.astype(F32)
        c0 += part.shape[1]
    y = cb_ref[...]
    for k in range(CONV_WIDTH):
        off = SUBLANES - (CONV_WIDTH - 1) + k
        y = y + buf_ref[off:off + n, :] * cw_ref[k:k + 1, :]
    buf_ref[0:SUBLANES, :] = buf_ref[n:n + SUBLANES, :]
    return y


def _ssd_kernel(z_ref, xs_ref, bc_ref, h_ref, wdt_ref, cw_ref, cb_ref, dtb_ref, alog_ref,
                dexp_ref, nw_ref, e_ref, o_ref, buf_ref, state_ref, wdtb_ref):
    c = pl.program_id(1)
    Q, N, G = CHUNK, SSM_STATE, SSM_GROUPS
    R = SSM_HEADS // G
    GW = D_BRANCH // G

    @pl.when(c == 0)
    def _():
        state_ref[...] = jnp.zeros_like(state_ref)
        wdtb_ref[...] = wdt_ref[...].astype(BF16)

    act = _silu(_causal_conv(buf_ref, [xs_ref[...], bc_ref[...]], cw_ref, cb_ref, c == 0))
    xs = act[:, :D_BRANCH]
    bm = act[:, D_BRANCH:D_BRANCH + G * N]
    cm = act[:, D_BRANCH + G * N:]

    head_lane = lax.broadcasted_iota(jnp.int32, (Q, LANES), 1) < SSM_HEADS
    dt_raw = lax.dot_general(h_ref[...], wdtb_ref[...], NT_DIMS, preferred_element_type=F32)
    dt = jnp.where(head_lane, _softplus(dt_raw + dtb_ref[...]), 0.0)
    da = dt * (-jnp.exp(alog_ref[...]))

    row = lax.broadcasted_iota(jnp.int32, (Q, Q), 0)
    col = lax.broadcasted_iota(jnp.int32, (Q, Q), 1)
    tri = row >= col
    lower = jnp.where(tri, 1.0, 0.0).astype(BF16)
    cs3 = jnp.dot(lower, jnp.concatenate([p.astype(BF16) for p in _split_bf16(da)], axis=1),
                  preferred_element_type=F32)
    da_cs = cs3[:, :LANES] + cs3[:, LANES:2 * LANES] + cs3[:, 2 * LANES:]
    cs2 = da_cs * LOG2E
    cs2T = cs2.T

    def expand(v):
        lhs = jnp.concatenate([p.astype(BF16) for p in _split_bf16(v)], axis=1)
        return jnp.dot(lhs, e_ref[...], preferred_element_type=F32)

    dt_exp = expand(dt)
    cs_exp = expand(da_cs)
    last_exp = cs_exp[Q - 1:Q, :]
    xdt = xs * dt_exp
    x_w = xdt * jnp.exp(last_exp - cs_exp)
    chunk_decay = jnp.exp(last_exp)
    in_decay = jnp.exp(cs_exp)

    lane = lax.broadcasted_iota(jnp.int32, (Q, GW), 1)
    for g in range(G):
        gs = slice(g * GW, (g + 1) * GW)
        bg = bm[:, g * N:(g + 1) * N]
        cgb = cm[:, g * N:(g + 1) * N].astype(BF16)
        bgT = bg.T.astype(BF16)
        cb = jnp.dot(cgb, bgT, preferred_element_type=F32)
        st = state_ref[:, gs]
        y_off = jnp.dot(cgb, st.astype(BF16), preferred_element_type=F32)
        state_ref[:, gs] = st * chunk_decay[:, gs] + jnp.dot(
            bgT, x_w[:, gs].astype(BF16), preferred_element_type=F32)
        xdt_g = xdt[:, gs]
        w_parts, x_parts = [], []
        for r in range(R):
            hd = g * R + r
            seg = cs2[:, hd:hd + 1] - cs2T[hd:hd + 1, :]
            lmat = jnp.exp2(jnp.where(tri, seg, -jnp.inf))
            w_parts.append((cb * lmat).astype(BF16))
            in_head = jnp.logical_and(lane >= r * SSM_HEAD_DIM, lane < (r + 1) * SSM_HEAD_DIM)
            x_parts.append(jnp.where(in_head, xdt_g, 0.0).astype(BF16))
        y_diag = jnp.dot(jnp.concatenate(w_parts, axis=1), jnp.concatenate(x_parts, axis=0),
                         preferred_element_type=F32)
        y = y_diag + y_off * in_decay[:, gs] + dexp_ref[:, gs] * xs[:, gs]
        y = y * _silu(z_ref[:, gs].astype(F32))
        ms = jnp.mean(y * y, axis=-1, keepdims=True)
        o_ref[:, gs] = (y * lax.rsqrt(ms + NORM_EPS) * nw_ref[:, gs]).astype(o_ref.dtype)


def _ssd(proj, hnorm, w_in, layer, conv_w, conv_b, dt_bias, a_log, d_skip, norm_w, batch, seq):
    nc = seq // CHUNK
    t = batch * seq
    d = hnorm.shape[1]
    H = SSM_HEADS
    assert COL_DT % LANES == 0 and H <= LANES
    pad = lambda v: jnp.pad(v.reshape(1, H), ((0, 0), (0, LANES - H)))
    expand = np.zeros((LANES, D_BRANCH), np.float32)
    for hd in range(H):
        expand[hd, hd * SSM_HEAD_DIM:(hd + 1) * SSM_HEAD_DIM] = 1.0
    d_exp = jnp.repeat(d_skip, SSM_HEAD_DIM).reshape(1, D_BRANCH)
    const = lambda shape: pl.BlockSpec(shape, lambda b, c: (0,) * len(shape))
    blk = lambda col0: pl.BlockSpec((CHUNK, D_BRANCH), lambda b, c: (b * nc + c, col0 // D_BRANCH))
    return pl.pallas_call(
        _ssd_kernel,
        grid=(batch, nc),
        in_specs=[blk(COL_Z), blk(COL_XS), blk(COL_BC),
                  pl.BlockSpec((CHUNK, d), lambda b, c: (b * nc + c, 0)),
                  pl.BlockSpec((None, LANES, d), lambda b, c: (layer, COL_DT // LANES, 0)),
                  const((CONV_WIDTH, SSM_CONV_DIM)), const((1, SSM_CONV_DIM)),
                  const((1, LANES)), const((1, LANES)),
                  const((1, D_BRANCH)), const((1, D_BRANCH)), const((BF16_TERMS * LANES, D_BRANCH))],
        out_specs=pl.BlockSpec((CHUNK, D_BRANCH), lambda b, c: (b * nc + c, 0)),
        out_shape=jax.ShapeDtypeStruct((t, D_BRANCH), BF16),
        scratch_shapes=[pltpu.VMEM((CHUNK + 2 * SUBLANES, SSM_CONV_DIM), F32),
                        pltpu.VMEM((SSM_STATE, D_BRANCH), F32),
                        pltpu.VMEM((LANES, d), BF16)],
        compiler_params=_params("parallel", "arbitrary"),
        name="ssd",
    )(proj, proj, proj, hnorm, w_in, conv_w, conv_b.reshape(1, -1),
      pad(dt_bias), pad(a_log),
      d_exp, norm_w.reshape(1, D_BRANCH), jnp.asarray(np.tile(expand, (BF16_TERMS, 1)), BF16))


def _lru_kernel(x_ref, g_ref, cw_ref, cb_ref, wa_ref, ba_ref, wx_ref, bx_ref, lam_ref,
                o_ref, buf_ref, carry_ref):
    c = pl.program_id(1)
    n = x_ref.shape[0]

    @pl.when(c == 0)
    def _():
        carry_ref[...] = jnp.zeros_like(carry_ref)

    xc = _causal_conv(buf_ref, [x_ref[...]], cw_ref, cb_ref, c == 0)
    ra, ix = [], []
    for blk in range(LRU_BLOCKS):
        xb = xc[:, blk * LRU_BLOCK_DIM:(blk + 1) * LRU_BLOCK_DIM].astype(BF16)
        ra.append(jnp.dot(xb, wa_ref[blk], preferred_element_type=F32))
        ix.append(jnp.dot(xb, wx_ref[blk], preferred_element_type=F32))
    r = _sigmoid(jnp.concatenate(ra, axis=1) + ba_ref[...])
    i = _sigmoid(jnp.concatenate(ix, axis=1) + bx_ref[...])
    log_a = -LRU_C * r * _softplus(-lam_ref[...])
    a = jnp.exp(log_a)
    u = jnp.sqrt(-jnp.tanh(log_a) * (1.0 + a * a)) * (i * xc)

    sub = jnp.bitwise_and(lax.broadcasted_iota(jnp.int32, a.shape, 0), SUBLANES - 1)
    d = 1
    while d < SUBLANES:
        keep = sub >= d
        a_sh = jnp.where(keep, pltpu.roll(a, d, 0), 1.0)
        u_sh = jnp.where(keep, pltpu.roll(u, d, 0), 0.0)
        u = a * u_sh + u
        a = a * a_sh
        d *= 2
    h = carry_ref[...]
    groups = []
    for r in range(0, n, SUBLANES):
        groups.append(u[r:r + SUBLANES, :] + a[r:r + SUBLANES, :] * h)
        h = groups[-1][SUBLANES - 1:SUBLANES, :]
    carry_ref[...] = h
    hseq = jnp.concatenate(groups, axis=0)
    o_ref[...] = (hseq * _silu(g_ref[...].astype(F32))).astype(o_ref.dtype)


def _lru(proj, conv_w, conv_b, w_a, b_a, w_x, b_x, lam, batch, seq):
    lc = CHUNK
    nc = seq // lc
    t = batch * seq
    const = lambda shape: pl.BlockSpec(shape, lambda b, c: (0,) * len(shape))
    wshape = (LRU_BLOCKS, LRU_BLOCK_DIM, LRU_BLOCK_DIM)
    return pl.pallas_call(
        _lru_kernel,
        grid=(batch, nc),
        in_specs=[pl.BlockSpec((lc, D_BRANCH), lambda b, c: (b * nc + c, 0)),
                  pl.BlockSpec((lc, D_BRANCH), lambda b, c: (b * nc + c, 1)),
                  const((CONV_WIDTH, D_BRANCH)), const((1, D_BRANCH)),
                  const(wshape), const((1, D_BRANCH)), const(wshape), const((1, D_BRANCH)),
                  const((1, D_BRANCH))],
        out_specs=pl.BlockSpec((lc, D_BRANCH), lambda b, c: (b * nc + c, 0)),
        out_shape=jax.ShapeDtypeStruct((t, D_BRANCH), BF16),
        scratch_shapes=[pltpu.VMEM((lc + 2 * SUBLANES, D_BRANCH), F32),
                        pltpu.VMEM((1, D_BRANCH), F32)],
        compiler_params=_params("parallel", "arbitrary"),
        name="rglru",
    )(proj, proj, conv_w, conv_b.reshape(1, -1), w_a.astype(BF16), b_a.reshape(1, -1),
      w_x.astype(BF16), b_x.reshape(1, -1), lam.reshape(1, -1))


def _outproj_kernel(y0_ref, y1_ref, y2_ref, y3_ref, w_ref, x_ref, o_ref, wb_ref):
    @pl.when(pl.program_id(1) == 0)
    def _():
        wb_ref[...] = w_ref[...].astype(BF16)

    acc = x_ref[...]
    for j, y_ref in enumerate((y0_ref, y1_ref, y2_ref, y3_ref)):
        acc = acc + jnp.dot(y_ref[...], wb_ref[j * D_BRANCH:(j + 1) * D_BRANCH, :],
                            preferred_element_type=F32)
    o_ref[...] = acc


def _outproj(ys, w, layer, x):
    t, d = x.shape
    k = w.shape[1]
    tm = min(1024, t)
    tn = 512
    yspec = pl.BlockSpec((tm, D_BRANCH), lambda j, i: (i, 0))
    return pl.pallas_call(
        _outproj_kernel,
        grid=(d // tn, t // tm),
        in_specs=[yspec, yspec, yspec, yspec,
                  pl.BlockSpec((None, k, tn), lambda j, i: (layer, 0, j)),
                  pl.BlockSpec((tm, tn), lambda j, i: (i, j))],
        out_specs=pl.BlockSpec((tm, tn), lambda j, i: (i, j)),
        out_shape=jax.ShapeDtypeStruct((t, d), F32),
        scratch_shapes=[pltpu.VMEM((k, tn), BF16)],
        compiler_params=_params("parallel", "arbitrary"),
        name="outproj",
    )(*ys, w, x)


def kernel(x, norm_w, w_in, ret_gn_w, ssm_conv_w, ssm_conv_b, ssm_dt_bias, ssm_a_log, ssm_d, ssm_norm_w, lru_conv_w, lru_conv_b, lru_w_a, lru_b_a, lru_w_x, lru_b_x, lru_lambda, w_out, final_norm_w):
    batch, seq, d = x.shape
    depth = w_in.shape[0]
    assert seq % CHUNK == 0 and d == 4 * D_BRANCH
    t = batch * seq
    xf = x.reshape(t, d)
    w_in_t = jnp.swapaxes(w_in, 1, 2)
    for l in range(depth):
        hn = _rmsnorm(xf, norm_w[l], BF16)
        proj = _inproj(hn, w_in_t, l, 0, D_MAIN)
        proj_lru = _inproj(hn, w_in_t, l, COL_LRU, 2 * D_BRANCH)
        y_ret = _retention(proj, ret_gn_w[l], batch, seq)
        y_moba = _moba(proj, batch, seq)
        y_ssm = _ssd(proj, hn, w_in_t, l, ssm_conv_w[l], ssm_conv_b[l],
                     ssm_dt_bias[l], ssm_a_log[l], ssm_d[l], ssm_norm_w[l], batch, seq)
        y_lru = _lru(proj_lru, lru_conv_w[l], lru_conv_b[l], lru_w_a[l], lru_b_a[l], lru_w_x[l],
                     lru_b_x[l], lru_lambda[l], batch, seq)
        xf = _outproj((y_ret, y_moba, y_ssm, y_lru), w_out, l, xf)
    return _rmsnorm(xf, final_norm_w, F32).reshape(batch, seq, d)
```

```python
import functools
import math

import jax
import jax.numpy as jnp
import numpy as np
from jax import lax
from jax.experimental import pallas as pl
from jax.experimental.pallas import tpu as pltpu

F32 = jnp.float32
BF16 = jnp.bfloat16

NORM_EPS = 1e-6
D_BRANCH = 1024
CHUNK = 256
RET_HEADS = 4
RET_HEAD_DIM = 256
MOBA_HEADS = 8
MOBA_HEAD_DIM = 128
MOBA_TOPK = 3
MOBA_HEADS_PER_STEP = 8
SSM_HEADS = 16
SSM_HEAD_DIM = 64
SSM_GROUPS = 4
SSM_STATE = 128
SSM_CONV_DIM = 2048
CONV_WIDTH = 4
LRU_BLOCKS = 8
LRU_BLOCK_DIM = 128
LRU_C = 8.0
LANES = 128
SUBLANES = 8
VMEM_LIMIT = 56 * 1024 * 1024

COL_RET = 0
COL_MOBA = 4 * D_BRANCH
COL_Z = 8 * D_BRANCH
COL_XS = COL_Z + D_BRANCH
COL_BC = COL_XS + D_BRANCH
COL_DT = COL_XS + SSM_CONV_DIM
COL_LRU = COL_DT + SSM_HEADS
D_MAIN = COL_DT

NT_DIMS = (((1,), (1,)), ((), ()))


def _params(*sem):
    return pltpu.CompilerParams(dimension_semantics=sem, vmem_limit_bytes=VMEM_LIMIT)


def _sigmoid(x):
    return 1.0 / (1.0 + jnp.exp(-x))


def _silu(x):
    return x * _sigmoid(x)


def _softplus(x):
    return jnp.maximum(x, 0.0) + jnp.log1p(jnp.exp(-jnp.abs(x)))


def _rmsnorm_kernel(x_ref, w_ref, o_ref):
    x = x_ref[...]
    ms = jnp.mean(x * x, axis=-1, keepdims=True)
    o_ref[...] = (x * lax.rsqrt(ms + NORM_EPS) * w_ref[...]).astype(o_ref.dtype)


def _rmsnorm(x, w, out_dtype):
    t, d = x.shape
    tm = min(512, t)
    return pl.pallas_call(
        _rmsnorm_kernel,
        grid=(t // tm,),
        in_specs=[pl.BlockSpec((tm, d), lambda i: (i, 0)),
                  pl.BlockSpec((1, d), lambda i: (0, 0))],
        out_specs=pl.BlockSpec((tm, d), lambda i: (i, 0)),
        out_shape=jax.ShapeDtypeStruct((t, d), out_dtype),
        compiler_params=_params("parallel"),
        name="rmsnorm",
    )(x, w.reshape(1, d))


def _inproj_kernel(a_ref, wt_ref, o_ref, wb_ref):
    @pl.when(pl.program_id(1) == 0)
    def _():
        wb_ref[...] = wt_ref[...].astype(BF16)

    o_ref[...] = lax.dot_general(a_ref[...], wb_ref[...], NT_DIMS,
                                 preferred_element_type=F32).astype(o_ref.dtype)


def _inproj(h, wt, layer, row0, n):
    t, k = h.shape
    tm = min(512, t)
    tn = 1024
    assert n % tn == 0 and row0 % SUBLANES == 0
    if row0 % tn == 0:
        wspec = pl.BlockSpec((None, tn, k), lambda j, i: (layer, row0 // tn + j, 0))
    else:
        wspec = pl.BlockSpec((None, pl.Element(tn), pl.Element(k)),
                             lambda j, i: (layer, pl.multiple_of(row0 + tn * j, SUBLANES), 0))
    return pl.pallas_call(
        _inproj_kernel,
        grid=(n // tn, t // tm),
        in_specs=[pl.BlockSpec((tm, k), lambda j, i: (i, 0)), wspec],
        out_specs=pl.BlockSpec((tm, tn), lambda j, i: (i, j)),
        out_shape=jax.ShapeDtypeStruct((t, n), BF16),
        scratch_shapes=[pltpu.VMEM((tn, k), BF16)],
        compiler_params=_params("parallel", "arbitrary"),
        name="inproj",
    )(h, wt)


def _retention_kernel(q_ref, k_ref, v_ref, g_ref, gn_ref, o_ref, state_ref):
    c = pl.program_id(1)

    @pl.when(c == 0)
    def _():
        state_ref[...] = jnp.zeros_like(state_ref)

    C, dh = CHUNK, RET_HEAD_DIM
    row = lax.broadcasted_iota(jnp.int32, (C, C), 0)
    col = lax.broadcasted_iota(jnp.int32, (C, C), 1)
    diff = (row - col).astype(F32)
    rowf = row.astype(F32)
    causal = row >= col
    for h in range(RET_HEADS):
        log_g = math.log1p(-(2.0 ** (-5.0 - h)))
        sl = slice(h * dh, (h + 1) * dh)
        dmat = jnp.where(causal, jnp.exp(log_g * jnp.where(causal, diff, 0.0)), 0.0)
        q_decay = jnp.exp(log_g * (rowf + 1.0))
        k_decay = jnp.exp(log_g * (C - 1.0 - rowf))
        qb = q_ref[:, sl]
        k = k_ref[:, sl].astype(F32) * (dh ** -0.5)
        kb = k.astype(BF16)
        vb = v_ref[:, sl]
        state = state_ref[h]
        inner = lax.dot_general(qb, kb, NT_DIMS, preferred_element_type=F32) * dmat
        y = jnp.dot(inner.astype(BF16), vb, preferred_element_type=F32)
        y = y + jnp.dot(qb, state.astype(BF16), preferred_element_type=F32) * q_decay
        kdT = (k * k_decay).T.astype(BF16)
        state_ref[h] = math.exp(log_g * C) * state + jnp.dot(kdT, vb, preferred_element_type=F32)
        ms = jnp.mean(y * y, axis=-1, keepdims=True)
        y = y * lax.rsqrt(ms + NORM_EPS) * gn_ref[h]
        o_ref[:, sl] = (y * _silu(g_ref[:, sl].astype(F32))).astype(o_ref.dtype)


def _retention(proj, gn_w, batch, seq):
    nc = seq // CHUNK
    t = batch * seq
    base = COL_RET // D_BRANCH

    def spec(j):
        return pl.BlockSpec((CHUNK, D_BRANCH), lambda b, c, j=j: (b * nc + c, base + j))

    return pl.pallas_call(
        _retention_kernel,
        grid=(batch, nc),
        in_specs=[spec(0), spec(1), spec(2), spec(3),
                  pl.BlockSpec((RET_HEADS, 1, RET_HEAD_DIM), lambda b, c: (0, 0, 0))],
        out_specs=pl.BlockSpec((CHUNK, D_BRANCH), lambda b, c: (b * nc + c, 0)),
        out_shape=jax.ShapeDtypeStruct((t, D_BRANCH), BF16),
        scratch_shapes=[pltpu.VMEM((RET_HEADS, RET_HEAD_DIM, RET_HEAD_DIM), F32)],
        compiler_params=_params("parallel", "arbitrary"),
        name="retention",
    )(proj, proj, proj, proj, gn_w.reshape(RET_HEADS, 1, RET_HEAD_DIM))


BF16_TERMS = 3
MOBA_VT_ROWS = MOBA_HEAD_DIM + 16
MOBA_M_INIT = -1e30
LOG2E = math.log2(math.e)


def _split_bf16(x):
    terms = []
    for _ in range(BF16_TERMS - 1):
        hi = x.astype(BF16).astype(F32)
        terms.append(hi)
        x = x - hi
    return terms + [x]


def _moba_kernel(slopes_ref, q_ref, k_ref, v_ref, g_ref, o_ref, kmean_ref, kx_ref, vt_ref, sel_ref,
                 s_ref, acc_ref, m_ref):
    hg = pl.program_id(1)
    i = pl.program_id(2)
    L, dh, HP = CHUNK, MOBA_HEAD_DIM, MOBA_HEADS_PER_STEP
    seq = k_ref.shape[0]
    nb = seq // L
    nbp = kmean_ref.shape[1]
    neg = -jnp.inf
    lane = lax.broadcasted_iota(jnp.int32, (L, LANES), 1)

    @pl.when(i == 0)
    def _():
        kmean_ref[...] = jnp.zeros_like(kmean_ref)
        vt_ref[:, :, dh:, :] = jnp.ones((HP, nb, MOBA_VT_ROWS - dh, L), BF16)
        within = lax.broadcasted_iota(jnp.int32, (L, LANES), 0).astype(F32)
        for n in range(nb):
            kx = jnp.where(lane < BF16_TERMS, within,
                           jnp.where(lane < 2 * BF16_TERMS, float(n), 0.0))
            kx_ref[n * L:(n + 1) * L, :] = kx.astype(BF16)
            for hh in range(HP):
                hs = slice(hh * dh, (hh + 1) * dh)
                kn = k_ref[n * L:(n + 1) * L, hs].astype(F32)
                kmean_ref[hh, n:n + 1, :] = jnp.mean(kn, axis=0, keepdims=True)
                vn = v_ref[n * L:(n + 1) * L, hs].astype(F32)
                vt_ref[hh, n, 0:dh, :] = vn.T.astype(BF16)

    blk_row = lax.broadcasted_iota(jnp.int32, (nbp, L), 0)
    blk_rowf = blk_row.astype(F32)
    key = lax.broadcasted_iota(jnp.int32, (L, L), 0)
    qry = lax.broadcasted_iota(jnp.int32, (L, L), 1)
    lane1 = lax.broadcasted_iota(jnp.int32, (1, LANES), 1)

    def scores(hh, qa, off):
        hs = slice(hh * dh, (hh + 1) * dh)
        ka = jnp.concatenate([k_ref[pl.ds(off, L), hs], kx_ref[pl.ds(off, L), :]], axis=1)
        return lax.dot_general(ka, qa, NT_DIMS, preferred_element_type=F32)

    qas = []
    for hh in range(HP):
        hs = slice(hh * dh, (hh + 1) * dh)
        qb = q_ref[:, hs]
        q = qb.astype(F32)
        km3 = jnp.concatenate([p.astype(BF16) for p in _split_bf16(kmean_ref[hh])], axis=0)
        gate3 = lax.dot_general(km3, qb, NT_DIMS, preferred_element_type=F32)
        gate = gate3[:nbp] + gate3[nbp:2 * nbp] + gate3[2 * nbp:]
        gm = jnp.where(blk_row < i, gate, neg)
        sel = jnp.zeros(gate.shape, F32)
        for _ in range(MOBA_TOPK):
            mx = jnp.max(gm, axis=0, keepdims=True)
            first = jnp.min(jnp.where(gm == mx, blk_rowf, float(nbp)), axis=0, keepdims=True)
            pick = jnp.logical_and(blk_rowf == first, mx > neg)
            sel = jnp.where(pick, 1.0, sel)
            gm = jnp.where(pick, neg, gm)
        sel_ref[hh] = sel
        slope2 = jnp.full((1, LANES), slopes_ref[hg * HP + hh] * LOG2E, F32)
        qx = jnp.zeros((1, LANES), F32)
        for t, term in enumerate(_split_bf16(slope2) + _split_bf16(slope2 * L)):
            qx = jnp.where(lane1 == t, term, qx)
        qa = jnp.concatenate([(q * (dh ** -0.5 * LOG2E)).astype(BF16),
                              jnp.broadcast_to(qx, (L, LANES)).astype(BF16)], axis=1)
        qas.append(qa)
        m_ref[hh] = jnp.full((1, L), MOBA_M_INIT, F32)
        acc_ref[hh] = jnp.zeros((MOBA_VT_ROWS, L), F32)
        s_ref[hh] = scores(hh, qa, 0)

    def update(hh, blk, s, picked):
        m = m_ref[hh]
        cmax = jnp.max(s, axis=0, keepdims=True)
        if picked is None:
            m_new = jnp.maximum(m, cmax)
            p = jnp.exp2(s - m_new)
        else:
            m_new = jnp.maximum(m, jnp.where(picked, cmax, neg))
            p = jnp.exp2(s - jnp.where(picked, m_new, jnp.inf))
        m_ref[hh] = m_new
        acc_ref[hh] = acc_ref[hh] * jnp.exp2(m - m_new) + jnp.dot(
            vt_ref[hh, blk], p.astype(BF16), preferred_element_type=F32)

    @pl.loop(0, i)
    def _(j):
        nxt = pl.multiple_of((j + 1) * L, L)
        for hh in range(HP):
            picked = sel_ref[hh, pl.ds(j, 1), :] > 0.0
            update(hh, j, s_ref[hh], picked)
            s_ref[hh] = scores(hh, qas[hh], nxt)

    for hh in range(HP):
        hs = slice(hh * dh, (hh + 1) * dh)
        update(hh, i, jnp.where(key <= qry, s_ref[hh], neg), None)
        acc = acc_ref[hh]
        out = (acc[:dh, :] / acc[dh:dh + 1, :]).T
        o_ref[:, hs] = (out * _silu(g_ref[:, hs].astype(F32))).astype(o_ref.dtype)


def _moba(proj, batch, seq):
    nb = seq // CHUNK
    t = batch * seq
    H, dh, HP = MOBA_HEADS, MOBA_HEAD_DIM, MOBA_HEADS_PER_STEP
    nbp = -(-nb // SUBLANES) * SUBLANES
    w = HP * dh
    base = COL_MOBA // w
    per = D_BRANCH // w
    slopes = jnp.asarray(np.exp2(-8.0 * (np.arange(H, dtype=np.float64) + 1.0) / H), F32)
    once = pl.Buffered(1)
    return pl.pallas_call(
        _moba_kernel,
        grid=(batch, H // HP, nb),
        in_specs=[pl.BlockSpec(memory_space=pltpu.SMEM),
                  pl.BlockSpec((CHUNK, w), lambda b, h, i: (b * nb + i, base + h)),
                  pl.BlockSpec((seq, w), lambda b, h, i: (b, base + per + h), pipeline_mode=once),
                  pl.BlockSpec((seq, w), lambda b, h, i: (b, base + 2 * per + h), pipeline_mode=once),
                  pl.BlockSpec((CHUNK, w), lambda b, h, i: (b * nb + i, base + 3 * per + h))],
        out_specs=pl.BlockSpec((CHUNK, w), lambda b, h, i: (b * nb + i, h)),
        out_shape=jax.ShapeDtypeStruct((t, D_BRANCH), BF16),
        scratch_shapes=[pltpu.VMEM((HP, nbp, dh), F32),
                        pltpu.VMEM((seq, LANES), BF16),
                        pltpu.VMEM((HP, nb, MOBA_VT_ROWS, CHUNK), BF16),
                        pltpu.VMEM((HP, nbp, CHUNK), F32),
                        pltpu.VMEM((HP, CHUNK, CHUNK), F32),
                        pltpu.VMEM((HP, MOBA_VT_ROWS, CHUNK), F32),
                        pltpu.VMEM((HP, 1, CHUNK), F32)],
        compiler_params=_params("parallel", "parallel", "arbitrary"),
        name="moba",
    )(slopes, proj, proj, proj, proj)


def _causal_conv(buf_ref, parts, cw_ref, cb_ref, first):
    n = parts[0].shape[0]

    @pl.when(first)
    def _():
        buf_ref[0:SUBLANES, :] = jnp.zeros((SUBLANES, buf_ref.shape[1]), F32)

    x = jnp.concatenate(parts, axis=1)
    xf = x.astype(F32)
    buf_ref[SUBLANES:2 * SUBLANES, :] = xf[0:SUBLANES, :]
    row = lax.broadcasted_iota(jnp.int32, (n, n), 0)
    col = lax.broadcasted_iota(jnp.int32, (n, n), 1)
    y = cb_ref[...] + xf * cw_ref[CONV_WIDTH - 1:CONV_WIDTH, :]
    head = cb_ref[...] + xf[0:SUBLANES, :] * cw_ref[CONV_WIDTH - 1:CONV_WIDTH, :]
    for k in range(CONV_WIDTH - 1):
        back = CONV_WIDTH - 1 - k
        shift = jnp.where(row - col == back, 1.0, 0.0).astype(BF16)
        y = y + jnp.dot(shift, x, preferred_element_type=F32) * cw_ref[k:k + 1, :]
        head = head + buf_ref[SUBLANES - back:2 * SUBLANES - back, :] * cw_ref[k:k + 1, :]
    buf_ref[0:SUBLANES, :] = xf[n - SUBLANES:n, :]
    return jnp.concatenate([head, y[SUBLANES:, :]], axis=0)


def _ssd_kernel(z_ref, xs_ref, bc_ref, h_ref, wdt_ref, cw_ref, cb_ref, dtb_ref, alog_ref,
                dexp_ref, nw_ref, e_ref, o_ref, buf_ref, state_ref, wdtb_ref):
    c = pl.program_id(1)
    Q, N, G = CHUNK, SSM_STATE, SSM_GROUPS
    R = SSM_HEADS // G
    GW = D_BRANCH // G

    @pl.when(c == 0)
    def _():
        state_ref[...] = jnp.zeros_like(state_ref)
        wdtb_ref[...] = wdt_ref[...].astype(BF16)

    act = _silu(_causal_conv(buf_ref, [xs_ref[...], bc_ref[...]], cw_ref, cb_ref, c == 0))
    xs = act[:, :D_BRANCH]
    bm = act[:, D_BRANCH:D_BRANCH + G * N]
    cm = act[:, D_BRANCH + G * N:]

    head_lane = lax.broadcasted_iota(jnp.int32, (Q, LANES), 1) < SSM_HEADS
    dt_raw = lax.dot_general(h_ref[...], wdtb_ref[...], NT_DIMS, preferred_element_type=F32)
    dt = jnp.where(head_lane, _softplus(dt_raw + dtb_ref[...]), 0.0)
    da = dt * (-jnp.exp(alog_ref[...]))

    row = lax.broadcasted_iota(jnp.int32, (Q, Q), 0)
    col = lax.broadcasted_iota(jnp.int32, (Q, Q), 1)
    tri = row >= col
    lower = jnp.where(tri, 1.0, 0.0).astype(BF16)
    cs3 = jnp.dot(lower, jnp.concatenate([p.astype(BF16) for p in _split_bf16(da)], axis=1),
                  preferred_element_type=F32)
    da_cs = cs3[:, :LANES] + cs3[:, LANES:2 * LANES] + cs3[:, 2 * LANES:]
    cs2 = da_cs * LOG2E
    cs2T = cs2.T

    def expand(v):
        lhs = jnp.concatenate([p.astype(BF16) for p in _split_bf16(v)], axis=1)
        return jnp.dot(lhs, e_ref[...], preferred_element_type=F32)

    dt_exp = expand(dt)
    cs_exp = expand(da_cs)
    last_exp = cs_exp[Q - 1:Q, :]
    xdt = xs * dt_exp
    x_w = xdt * jnp.exp(last_exp - cs_exp)
    chunk_decay = jnp.exp(last_exp)
    in_decay = jnp.exp(cs_exp)

    lane = lax.broadcasted_iota(jnp.int32, (Q, GW), 1)
    for g in range(G):
        gs = slice(g * GW, (g + 1) * GW)
        bg = bm[:, g * N:(g + 1) * N]
        cgb = cm[:, g * N:(g + 1) * N].astype(BF16)
        bgT = bg.T.astype(BF16)
        cb = jnp.dot(cgb, bgT, preferred_element_type=F32)
        st = state_ref[:, gs]
        y_off = jnp.dot(cgb, st.astype(BF16), preferred_element_type=F32)
        state_ref[:, gs] = st * chunk_decay[:, gs] + jnp.dot(
            bgT, x_w[:, gs].astype(BF16), preferred_element_type=F32)
        xdt_g = xdt[:, gs]
        w_parts, x_parts = [], []
        for r in range(R):
            hd = g * R + r
            seg = cs2[:, hd:hd + 1] - cs2T[hd:hd + 1, :]
            lmat = jnp.exp2(jnp.where(tri, seg, -jnp.inf))
            w_parts.append((cb * lmat).astype(BF16))
            in_head = jnp.logical_and(lane >= r * SSM_HEAD_DIM, lane < (r + 1) * SSM_HEAD_DIM)
            x_parts.append(jnp.where(in_head, xdt_g, 0.0).astype(BF16))
        y_diag = jnp.dot(jnp.concatenate(w_parts, axis=1), jnp.concatenate(x_parts, axis=0),
                         preferred_element_type=F32)
        y = y_diag + y_off * in_decay[:, gs] + dexp_ref[:, gs] * xs[:, gs]
        y = y * _silu(z_ref[:, gs].astype(F32))
        ms = jnp.mean(y * y, axis=-1, keepdims=True)
        o_ref[:, gs] = (y * lax.rsqrt(ms + NORM_EPS) * nw_ref[:, gs]).astype(o_ref.dtype)


def _ssd(proj, hnorm, w_in, layer, conv_w, conv_b, dt_bias, a_log, d_skip, norm_w, batch, seq):
    nc = seq // CHUNK
    t = batch * seq
    d = hnorm.shape[1]
    H = SSM_HEADS
    assert COL_DT % LANES == 0 and H <= LANES
    pad = lambda v: jnp.pad(v.reshape(1, H), ((0, 0), (0, LANES - H)))
    expand = np.zeros((LANES, D_BRANCH), np.float32)
    for hd in range(H):
        expand[hd, hd * SSM_HEAD_DIM:(hd + 1) * SSM_HEAD_DIM] = 1.0
    d_exp = jnp.repeat(d_skip, SSM_HEAD_DIM).reshape(1, D_BRANCH)
    const = lambda shape: pl.BlockSpec(shape, lambda b, c: (0,) * len(shape))
    blk = lambda col0: pl.BlockSpec((CHUNK, D_BRANCH), lambda b, c: (b * nc + c, col0 // D_BRANCH))
    return pl.pallas_call(
        _ssd_kernel,
        grid=(batch, nc),
        in_specs=[blk(COL_Z), blk(COL_XS), blk(COL_BC),
                  pl.BlockSpec((CHUNK, d), lambda b, c: (b * nc + c, 0)),
                  pl.BlockSpec((None, LANES, d), lambda b, c: (layer, COL_DT // LANES, 0)),
                  const((CONV_WIDTH, SSM_CONV_DIM)), const((1, SSM_CONV_DIM)),
                  const((1, LANES)), const((1, LANES)),
                  const((1, D_BRANCH)), const((1, D_BRANCH)), const((BF16_TERMS * LANES, D_BRANCH))],
        out_specs=pl.BlockSpec((CHUNK, D_BRANCH), lambda b, c: (b * nc + c, 0)),
        out_shape=jax.ShapeDtypeStruct((t, D_BRANCH), BF16),
        scratch_shapes=[pltpu.VMEM((2 * SUBLANES, SSM_CONV_DIM), F32),
                        pltpu.VMEM((SSM_STATE, D_BRANCH), F32),
                        pltpu.VMEM((LANES, d), BF16)],
        compiler_params=_params("parallel", "arbitrary"),
        name="ssd",
    )(proj, proj, proj, hnorm, w_in, conv_w, conv_b.reshape(1, -1),
      pad(dt_bias), pad(a_log),
      d_exp, norm_w.reshape(1, D_BRANCH), jnp.asarray(np.tile(expand, (BF16_TERMS, 1)), BF16))


def _lru_kernel(x_ref, g_ref, cw_ref, cb_ref, wa_ref, ba_ref, wx_ref, bx_ref, lam_ref,
                o_ref, buf_ref, carry_ref):
    c = pl.program_id(1)
    n = x_ref.shape[0]

    @pl.when(c == 0)
    def _():
        carry_ref[...] = jnp.zeros_like(carry_ref)

    xc = _causal_conv(buf_ref, [x_ref[...]], cw_ref, cb_ref, c == 0)
    ra, ix = [], []
    for blk in range(LRU_BLOCKS):
        xb = xc[:, blk * LRU_BLOCK_DIM:(blk + 1) * LRU_BLOCK_DIM].astype(BF16)
        ra.append(jnp.dot(xb, wa_ref[blk], preferred_element_type=F32))
        ix.append(jnp.dot(xb, wx_ref[blk], preferred_element_type=F32))
    r = _sigmoid(jnp.concatenate(ra, axis=1) + ba_ref[...])
    i = _sigmoid(jnp.concatenate(ix, axis=1) + bx_ref[...])
    log_a = -LRU_C * r * _softplus(-lam_ref[...])
    a = jnp.exp(log_a)
    u = jnp.sqrt(-jnp.tanh(log_a) * (1.0 + a * a)) * (i * xc)

    width = a.shape[1]
    a = a.reshape(n // SUBLANES, SUBLANES, width)
    u = u.reshape(n // SUBLANES, SUBLANES, width)
    sub = lax.broadcasted_iota(jnp.int32, (1, SUBLANES, width), 1)
    d = 1
    while d < SUBLANES:
        keep = sub >= d
        a_sh = jnp.where(keep, pltpu.roll(a, d, 1), 1.0)
        u_sh = jnp.where(keep, pltpu.roll(u, d, 1), 0.0)
        u = a * u_sh + u
        a = a * a_sh
        d *= 2
    h = carry_ref[...]
    groups = []
    for r in range(n // SUBLANES):
        groups.append(u[r] + a[r] * h)
        h = groups[-1][SUBLANES - 1:SUBLANES, :]
    carry_ref[...] = h
    hseq = jnp.concatenate(groups, axis=0)
    o_ref[...] = (hseq * _silu(g_ref[...].astype(F32))).astype(o_ref.dtype)


def _lru(proj, conv_w, conv_b, w_a, b_a, w_x, b_x, lam, batch, seq):
    lc = CHUNK
    nc = seq // lc
    t = batch * seq
    const = lambda shape: pl.BlockSpec(shape, lambda b, c: (0,) * len(shape))
    wshape = (LRU_BLOCKS, LRU_BLOCK_DIM, LRU_BLOCK_DIM)
    return pl.pallas_call(
        _lru_kernel,
        grid=(batch, nc),
        in_specs=[pl.BlockSpec((lc, D_BRANCH), lambda b, c: (b * nc + c, 0)),
                  pl.BlockSpec((lc, D_BRANCH), lambda b, c: (b * nc + c, 1)),
                  const((CONV_WIDTH, D_BRANCH)), const((1, D_BRANCH)),
                  const(wshape), const((1, D_BRANCH)), const(wshape), const((1, D_BRANCH)),
                  const((1, D_BRANCH))],
        out_specs=pl.BlockSpec((lc, D_BRANCH), lambda b, c: (b * nc + c, 0)),
        out_shape=jax.ShapeDtypeStruct((t, D_BRANCH), BF16),
        scratch_shapes=[pltpu.VMEM((2 * SUBLANES, D_BRANCH), F32),
                        pltpu.VMEM((1, D_BRANCH), F32)],
        compiler_params=_params("parallel", "arbitrary"),
        name="rglru",
    )(proj, proj, conv_w, conv_b.reshape(1, -1), w_a.astype(BF16), b_a.reshape(1, -1),
      w_x.astype(BF16), b_x.reshape(1, -1), lam.reshape(1, -1))


def _outproj_kernel(y0_ref, y1_ref, y2_ref, y3_ref, w_ref, x_ref, o_ref, wb_ref):
    @pl.when(pl.program_id(1) == 0)
    def _():
        wb_ref[...] = w_ref[...].astype(BF16)

    acc = x_ref[...]
    for j, y_ref in enumerate((y0_ref, y1_ref, y2_ref, y3_ref)):
        acc = acc + jnp.dot(y_ref[...], wb_ref[j * D_BRANCH:(j + 1) * D_BRANCH, :],
                            preferred_element_type=F32)
    o_ref[...] = acc


def _outproj(ys, w, layer, x):
    t, d = x.shape
    k = w.shape[1]
    tm = min(1024, t)
    tn = 512
    yspec = pl.BlockSpec((tm, D_BRANCH), lambda j, i: (i, 0))
    return pl.pallas_call(
        _outproj_kernel,
        grid=(d // tn, t // tm),
        in_specs=[yspec, yspec, yspec, yspec,
                  pl.BlockSpec((None, k, tn), lambda j, i: (layer, 0, j)),
                  pl.BlockSpec((tm, tn), lambda j, i: (i, j))],
        out_specs=pl.BlockSpec((tm, tn), lambda j, i: (i, j)),
        out_shape=jax.ShapeDtypeStruct((t, d), F32),
        scratch_shapes=[pltpu.VMEM((k, tn), BF16)],
        compiler_params=_params("parallel", "arbitrary"),
        name="outproj",
    )(*ys, w, x)


def kernel(x, norm_w, w_in, ret_gn_w, ssm_conv_w, ssm_conv_b, ssm_dt_bias, ssm_a_log, ssm_d, ssm_norm_w, lru_conv_w, lru_conv_b, lru_w_a, lru_b_a, lru_w_x, lru_b_x, lru_lambda, w_out, final_norm_w):
    batch, seq, d = x.shape
    depth = w_in.shape[0]
    assert seq % CHUNK == 0 and d == 4 * D_BRANCH
    t = batch * seq
    xf = x.reshape(t, d)
    w_in_t = jnp.swapaxes(w_in, 1, 2)
    for l in range(depth):
        hn = _rmsnorm(xf, norm_w[l], BF16)
        proj = _inproj(hn, w_in_t, l, 0, D_MAIN)
        proj_lru = _inproj(hn, w_in_t, l, COL_LRU, 2 * D_BRANCH)
        y_ret = _retention(proj, ret_gn_w[l], batch, seq)
        y_moba = _moba(proj, batch, seq)
        y_ssm = _ssd(proj, hn, w_in_t, l, ssm_conv_w[l], ssm_conv_b[l],
                     ssm_dt_bias[l], ssm_a_log[l], ssm_d[l], ssm_norm_w[l], batch, seq)
        y_lru = _lru(proj_lru, lru_conv_w[l], lru_conv_b[l], lru_w_a[l], lru_b_a[l], lru_w_x[l],
                     lru_b_x[l], lru_lambda[l], batch, seq)
        xf = _outproj((y_ret, y_moba, y_ssm, y_lru), w_out, l, xf)
    return _rmsnorm(xf, final_norm_w, F32).reshape(batch, seq, d)
```

```python
import functools
import math

import jax
import jax.numpy as jnp
import numpy as np
from jax import lax
from jax.experimental import pallas as pl
from jax.experimental.pallas import tpu as pltpu

F32 = jnp.float32
BF16 = jnp.bfloat16

NORM_EPS = 1e-6
D_BRANCH = 1024
CHUNK = 256
RET_HEADS = 4
RET_HEAD_DIM = 256
MOBA_HEADS = 8
MOBA_HEAD_DIM = 128
MOBA_TOPK = 3
MOBA_HEADS_PER_STEP = 8
SSM_HEADS = 16
SSM_HEAD_DIM = 64
SSM_GROUPS = 4
SSM_STATE = 128
SSM_CONV_DIM = 2048
CONV_WIDTH = 4
LRU_BLOCKS = 8
LRU_BLOCK_DIM = 128
LRU_C = 8.0
LANES = 128
SUBLANES = 8
VMEM_LIMIT = 56 * 1024 * 1024

COL_RET = 0
COL_MOBA = 4 * D_BRANCH
COL_Z = 8 * D_BRANCH
COL_XS = COL_Z + D_BRANCH
COL_BC = COL_XS + D_BRANCH
COL_DT = COL_XS + SSM_CONV_DIM
COL_LRU = COL_DT + SSM_HEADS
D_MAIN = COL_DT

NT_DIMS = (((1,), (1,)), ((), ()))


def _params(*sem):
    return pltpu.CompilerParams(dimension_semantics=sem, vmem_limit_bytes=VMEM_LIMIT)


def _sigmoid(x):
    return 1.0 / (1.0 + jnp.exp(-x))


def _silu(x):
    return x * _sigmoid(x)


def _softplus(x):
    return jnp.maximum(x, 0.0) + jnp.log1p(jnp.exp(-jnp.abs(x)))


def _rmsnorm_kernel(x_ref, w_ref, o_ref):
    x = x_ref[...]
    ms = jnp.mean(x * x, axis=-1, keepdims=True)
    o_ref[...] = (x * lax.rsqrt(ms + NORM_EPS) * w_ref[...]).astype(o_ref.dtype)


def _rmsnorm(x, w, out_dtype):
    t, d = x.shape
    tm = min(512, t)
    return pl.pallas_call(
        _rmsnorm_kernel,
        grid=(t // tm,),
        in_specs=[pl.BlockSpec((tm, d), lambda i: (i, 0)),
                  pl.BlockSpec((1, d), lambda i: (0, 0))],
        out_specs=pl.BlockSpec((tm, d), lambda i: (i, 0)),
        out_shape=jax.ShapeDtypeStruct((t, d), out_dtype),
        compiler_params=_params("parallel"),
        name="rmsnorm",
    )(x, w.reshape(1, d))


def _inproj_kernel(a_ref, wt_ref, o_ref, wb_ref):
    @pl.when(pl.program_id(1) == 0)
    def _():
        wb_ref[...] = wt_ref[...].astype(BF16)

    o_ref[...] = lax.dot_general(a_ref[...], wb_ref[...], NT_DIMS,
                                 preferred_element_type=F32).astype(o_ref.dtype)


def _inproj(h, wt, layer, row0, n):
    t, k = h.shape
    tm = min(512, t)
    tn = 1024
    assert n % tn == 0 and row0 % SUBLANES == 0
    if row0 % tn == 0:
        wspec = pl.BlockSpec((None, tn, k), lambda j, i: (layer, row0 // tn + j, 0))
    else:
        wspec = pl.BlockSpec((None, pl.Element(tn), pl.Element(k)),
                             lambda j, i: (layer, pl.multiple_of(row0 + tn * j, SUBLANES), 0))
    return pl.pallas_call(
        _inproj_kernel,
        grid=(n // tn, t // tm),
        in_specs=[pl.BlockSpec((tm, k), lambda j, i: (i, 0)), wspec],
        out_specs=pl.BlockSpec((tm, tn), lambda j, i: (i, j)),
        out_shape=jax.ShapeDtypeStruct((t, n), BF16),
        scratch_shapes=[pltpu.VMEM((tn, k), BF16)],
        compiler_params=_params("parallel", "arbitrary"),
        name="inproj",
    )(h, wt)


def _retention_kernel(q_ref, k_ref, v_ref, g_ref, gn_ref, o_ref, state_ref):
    c = pl.program_id(1)

    @pl.when(c == 0)
    def _():
        state_ref[...] = jnp.zeros_like(state_ref)

    C, dh = CHUNK, RET_HEAD_DIM
    row = lax.broadcasted_iota(jnp.int32, (C, C), 0)
    col = lax.broadcasted_iota(jnp.int32, (C, C), 1)
    diff = (row - col).astype(F32)
    rowf = row.astype(F32)
    causal = row >= col
    for h in range(RET_HEADS):
        log_g = math.log1p(-(2.0 ** (-5.0 - h)))
        sl = slice(h * dh, (h + 1) * dh)
        dmat = jnp.where(causal, jnp.exp(log_g * jnp.where(causal, diff, 0.0)), 0.0)
        q_decay = jnp.exp(log_g * (rowf + 1.0))
        k_decay = jnp.exp(log_g * (C - 1.0 - rowf))
        qb = q_ref[:, sl]
        k = k_ref[:, sl].astype(F32) * (dh ** -0.5)
        kb = k.astype(BF16)
        vb = v_ref[:, sl]
        state = state_ref[h]
        inner = lax.dot_general(qb, kb, NT_DIMS, preferred_element_type=F32) * dmat
        y = jnp.dot(inner.astype(BF16), vb, preferred_element_type=F32)
        y = y + jnp.dot(qb, state.astype(BF16), preferred_element_type=F32) * q_decay
        kdT = (k * k_decay).T.astype(BF16)
        state_ref[h] = math.exp(log_g * C) * state + jnp.dot(kdT, vb, preferred_element_type=F32)
        ms = jnp.mean(y * y, axis=-1, keepdims=True)
        y = y * lax.rsqrt(ms + NORM_EPS) * gn_ref[h]
        o_ref[:, sl] = (y * _silu(g_ref[:, sl].astype(F32))).astype(o_ref.dtype)


def _retention(proj, gn_w, batch, seq):
    nc = seq // CHUNK
    t = batch * seq
    base = COL_RET // D_BRANCH

    def spec(j):
        return pl.BlockSpec((CHUNK, D_BRANCH), lambda b, c, j=j: (b * nc + c, base + j))

    return pl.pallas_call(
        _retention_kernel,
        grid=(batch, nc),
        in_specs=[spec(0), spec(1), spec(2), spec(3),
                  pl.BlockSpec((RET_HEADS, 1, RET_HEAD_DIM), lambda b, c: (0, 0, 0))],
        out_specs=pl.BlockSpec((CHUNK, D_BRANCH), lambda b, c: (b * nc + c, 0)),
        out_shape=jax.ShapeDtypeStruct((t, D_BRANCH), BF16),
        scratch_shapes=[pltpu.VMEM((RET_HEADS, RET_HEAD_DIM, RET_HEAD_DIM), F32)],
        compiler_params=_params("parallel", "arbitrary"),
        name="retention",
    )(proj, proj, proj, proj, gn_w.reshape(RET_HEADS, 1, RET_HEAD_DIM))


BF16_TERMS = 3
MOBA_VT_ROWS = MOBA_HEAD_DIM + 16
MOBA_M_INIT = -1e30
LOG2E = math.log2(math.e)


def _split_bf16(x):
    terms = []
    for _ in range(BF16_TERMS - 1):
        hi = x.astype(BF16).astype(F32)
        terms.append(hi)
        x = x - hi
    return terms + [x]


def _moba_kernel(slopes_ref, q_ref, k_ref, v_ref, g_ref, o_ref, kmean_ref, kx_ref, vt_ref, sel_ref,
                 s_ref, acc_ref, m_ref):
    hg = pl.program_id(1)
    i = pl.program_id(2)
    L, dh, HP = CHUNK, MOBA_HEAD_DIM, MOBA_HEADS_PER_STEP
    seq = k_ref.shape[0]
    nb = seq // L
    nbp = kmean_ref.shape[1]
    neg = -jnp.inf
    lane = lax.broadcasted_iota(jnp.int32, (L, LANES), 1)

    @pl.when(i == 0)
    def _():
        kmean_ref[...] = jnp.zeros_like(kmean_ref)
        vt_ref[:, :, dh:, :] = jnp.ones((HP, nb, MOBA_VT_ROWS - dh, L), BF16)
        within = lax.broadcasted_iota(jnp.int32, (L, LANES), 0).astype(F32)
        for n in range(nb):
            kx = jnp.where(lane < BF16_TERMS, within,
                           jnp.where(lane < 2 * BF16_TERMS, float(n), 0.0))
            kx_ref[n * L:(n + 1) * L, :] = kx.astype(BF16)
            for hh in range(HP):
                hs = slice(hh * dh, (hh + 1) * dh)
                kn = k_ref[n * L:(n + 1) * L, hs].astype(F32)
                kmean_ref[hh, n:n + 1, :] = jnp.mean(kn, axis=0, keepdims=True)
                vn = v_ref[n * L:(n + 1) * L, hs].astype(F32)
                vt_ref[hh, n, 0:dh, :] = vn.T.astype(BF16)

    blk_row = lax.broadcasted_iota(jnp.int32, (nbp, L), 0)
    blk_rowf = blk_row.astype(F32)
    key = lax.broadcasted_iota(jnp.int32, (L, L), 0)
    qry = lax.broadcasted_iota(jnp.int32, (L, L), 1)
    lane1 = lax.broadcasted_iota(jnp.int32, (1, LANES), 1)

    def scores(hh, qa, off):
        hs = slice(hh * dh, (hh + 1) * dh)
        ka = jnp.concatenate([k_ref[pl.ds(off, L), hs], kx_ref[pl.ds(off, L), :]], axis=1)
        return lax.dot_general(ka, qa, NT_DIMS, preferred_element_type=F32)

    qas = []
    for hh in range(HP):
        hs = slice(hh * dh, (hh + 1) * dh)
        qb = q_ref[:, hs]
        q = qb.astype(F32)
        km3 = jnp.concatenate([p.astype(BF16) for p in _split_bf16(kmean_ref[hh])], axis=0)
        gate3 = lax.dot_general(km3, qb, NT_DIMS, preferred_element_type=F32)
        gate = gate3[:nbp] + gate3[nbp:2 * nbp] + gate3[2 * nbp:]
        gm = jnp.where(blk_row < i, gate, neg)
        sel = jnp.zeros(gate.shape, F32)
        for _ in range(MOBA_TOPK):
            mx = jnp.max(gm, axis=0, keepdims=True)
            first = jnp.min(jnp.where(gm == mx, blk_rowf, float(nbp)), axis=0, keepdims=True)
            pick = jnp.logical_and(blk_rowf == first, mx > neg)
            sel = jnp.where(pick, 1.0, sel)
            gm = jnp.where(pick, neg, gm)
        sel_ref[hh] = sel
        slope2 = jnp.full((1, LANES), slopes_ref[hg * HP + hh] * LOG2E, F32)
        qx = jnp.zeros((1, LANES), F32)
        for t, term in enumerate(_split_bf16(slope2) + _split_bf16(slope2 * L)):
            qx = jnp.where(lane1 == t, term, qx)
        qa = jnp.concatenate([(q * (dh ** -0.5 * LOG2E)).astype(BF16),
                              jnp.broadcast_to(qx, (L, LANES)).astype(BF16)], axis=1)
        qas.append(qa)
        m_ref[hh] = jnp.full((1, L), MOBA_M_INIT, F32)
        acc_ref[hh] = jnp.zeros((MOBA_VT_ROWS, L), F32)
        s_ref[0, hh] = scores(hh, qa, 0)
        s_ref[1, hh] = scores(hh, qa, pl.multiple_of(jnp.minimum(i, 1) * L, L))

    def update(hh, blk, s, picked):
        m = m_ref[hh]
        cmax = jnp.max(s, axis=0, keepdims=True)
        if picked is None:
            m_new = jnp.maximum(m, cmax)
            p = jnp.exp2(s - m_new)
        else:
            m_new = jnp.maximum(m, jnp.where(picked, cmax, neg))
            p = jnp.exp2(s - jnp.where(picked, m_new, jnp.inf))
        m_ref[hh] = m_new
        acc_ref[hh] = acc_ref[hh] * jnp.exp2(m - m_new) + jnp.dot(
            vt_ref[hh, blk], p.astype(BF16), preferred_element_type=F32)

    def past_block(hh, blk, slot):
        picked = sel_ref[hh, pl.ds(blk, 1), :] > 0.0
        update(hh, blk, s_ref[slot, hh], picked)

    @pl.loop(0, jnp.right_shift(i, 1))
    def _(pair):
        blk = 2 * pair
        nxt0 = pl.multiple_of((blk + 2) * L, L)
        nxt1 = pl.multiple_of(jnp.minimum(blk + 3, i) * L, L)
        for hh in range(HP):
            past_block(hh, blk, 0)
            s_ref[0, hh] = scores(hh, qas[hh], nxt0)
            past_block(hh, blk + 1, 1)
            s_ref[1, hh] = scores(hh, qas[hh], nxt1)

    odd = jnp.bitwise_and(i, 1) == 1

    @pl.when(odd)
    def _():
        for hh in range(HP):
            past_block(hh, i - 1, 0)

    for hh in range(HP):
        hs = slice(hh * dh, (hh + 1) * dh)
        s_own = jnp.where(odd, s_ref[1, hh], s_ref[0, hh])
        update(hh, i, jnp.where(key <= qry, s_own, neg), None)
        acc = acc_ref[hh]
        out = (acc[:dh, :] / acc[dh:dh + 1, :]).T
        o_ref[:, hs] = (out * _silu(g_ref[:, hs].astype(F32))).astype(o_ref.dtype)


def _moba(proj, batch, seq):
    nb = seq // CHUNK
    t = batch * seq
    H, dh, HP = MOBA_HEADS, MOBA_HEAD_DIM, MOBA_HEADS_PER_STEP
    nbp = -(-nb // SUBLANES) * SUBLANES
    w = HP * dh
    base = COL_MOBA // w
    per = D_BRANCH // w
    slopes = jnp.asarray(np.exp2(-8.0 * (np.arange(H, dtype=np.float64) + 1.0) / H), F32)
    once = pl.Buffered(1)
    return pl.pallas_call(
        _moba_kernel,
        grid=(batch, H // HP, nb),
        in_specs=[pl.BlockSpec(memory_space=pltpu.SMEM),
                  pl.BlockSpec((CHUNK, w), lambda b, h, i: (b * nb + i, base + h)),
                  pl.BlockSpec((seq, w), lambda b, h, i: (b, base + per + h), pipeline_mode=once),
                  pl.BlockSpec((seq, w), lambda b, h, i: (b, base + 2 * per + h), pipeline_mode=once),
                  pl.BlockSpec((CHUNK, w), lambda b, h, i: (b * nb + i, base + 3 * per + h))],
        out_specs=pl.BlockSpec((CHUNK, w), lambda b, h, i: (b * nb + i, h)),
        out_shape=jax.ShapeDtypeStruct((t, D_BRANCH), BF16),
        scratch_shapes=[pltpu.VMEM((HP, nbp, dh), F32),
                        pltpu.VMEM((seq, LANES), BF16),
                        pltpu.VMEM((HP, nb, MOBA_VT_ROWS, CHUNK), BF16),
                        pltpu.VMEM((HP, nbp, CHUNK), F32),
                        pltpu.VMEM((2, HP, CHUNK, CHUNK), F32),
                        pltpu.VMEM((HP, MOBA_VT_ROWS, CHUNK), F32),
                        pltpu.VMEM((HP, 1, CHUNK), F32)],
        compiler_params=_params("parallel", "parallel", "arbitrary"),
        name="moba",
    )(slopes, proj, proj, proj, proj)


def _causal_conv(buf_ref, parts, cw_ref, cb_ref, first):
    n = parts[0].shape[0]

    @pl.when(first)
    def _():
        buf_ref[0:SUBLANES, :] = jnp.zeros((SUBLANES, buf_ref.shape[1]), F32)

    x = jnp.concatenate(parts, axis=1)
    xf = x.astype(F32)
    buf_ref[SUBLANES:2 * SUBLANES, :] = xf[0:SUBLANES, :]
    row = lax.broadcasted_iota(jnp.int32, (n, n), 0)
    col = lax.broadcasted_iota(jnp.int32, (n, n), 1)
    y = cb_ref[...] + xf * cw_ref[CONV_WIDTH - 1:CONV_WIDTH, :]
    head = cb_ref[...] + xf[0:SUBLANES, :] * cw_ref[CONV_WIDTH - 1:CONV_WIDTH, :]
    for k in range(CONV_WIDTH - 1):
        back = CONV_WIDTH - 1 - k
        shift = jnp.where(row - col == back, 1.0, 0.0).astype(BF16)
        y = y + jnp.dot(shift, x, preferred_element_type=F32) * cw_ref[k:k + 1, :]
        head = head + buf_ref[SUBLANES - back:2 * SUBLANES - back, :] * cw_ref[k:k + 1, :]
    buf_ref[0:SUBLANES, :] = xf[n - SUBLANES:n, :]
    return jnp.concatenate([head, y[SUBLANES:, :]], axis=0)


def _ssd_kernel(z_ref, xs_ref, bc_ref, h_ref, wdt_ref, cw_ref, cb_ref, dtb_ref, alog_ref,
                dexp_ref, nw_ref, e_ref, o_ref, buf_ref, state_ref, wdtb_ref):
    c = pl.program_id(1)
    Q, N, G = CHUNK, SSM_STATE, SSM_GROUPS
    R = SSM_HEADS // G
    GW = D_BRANCH // G

    @pl.when(c == 0)
    def _():
        state_ref[...] = jnp.zeros_like(state_ref)
        wdtb_ref[...] = wdt_ref[...].astype(BF16)

    act = _silu(_causal_conv(buf_ref, [xs_ref[...], bc_ref[...]], cw_ref, cb_ref, c == 0))
    xs = act[:, :D_BRANCH]
    bm = act[:, D_BRANCH:D_BRANCH + G * N]
    cm = act[:, D_BRANCH + G * N:]

    head_lane = lax.broadcasted_iota(jnp.int32, (Q, LANES), 1) < SSM_HEADS
    dt_raw = lax.dot_general(h_ref[...], wdtb_ref[...], NT_DIMS, preferred_element_type=F32)
    dt = jnp.where(head_lane, _softplus(dt_raw + dtb_ref[...]), 0.0)
    da = dt * (-jnp.exp(alog_ref[...]))

    row = lax.broadcasted_iota(jnp.int32, (Q, Q), 0)
    col = lax.broadcasted_iota(jnp.int32, (Q, Q), 1)
    tri = row >= col
    lower = jnp.where(tri, 1.0, 0.0).astype(BF16)
    cs3 = jnp.dot(lower, jnp.concatenate([p.astype(BF16) for p in _split_bf16(da)], axis=1),
                  preferred_element_type=F32)
    da_cs = cs3[:, :LANES] + cs3[:, LANES:2 * LANES] + cs3[:, 2 * LANES:]
    cs2 = da_cs * LOG2E
    cs2T = cs2.T

    def expand(v):
        lhs = jnp.concatenate([p.astype(BF16) for p in _split_bf16(v)], axis=1)
        return jnp.dot(lhs, e_ref[...], preferred_element_type=F32)

    dt_exp = expand(dt)
    cs_exp = expand(da_cs)
    last_exp = cs_exp[Q - 1:Q, :]
    xdt = xs * dt_exp
    x_w = xdt * jnp.exp(last_exp - cs_exp)
    chunk_decay = jnp.exp(last_exp)
    in_decay = jnp.exp(cs_exp)

    lane = lax.broadcasted_iota(jnp.int32, (Q, GW), 1)
    for g in range(G):
        gs = slice(g * GW, (g + 1) * GW)
        bg = bm[:, g * N:(g + 1) * N]
        cgb = cm[:, g * N:(g + 1) * N].astype(BF16)
        bgT = bg.T.astype(BF16)
        cb = jnp.dot(cgb, bgT, preferred_element_type=F32)
        st = state_ref[:, gs]
        y_off = jnp.dot(cgb, st.astype(BF16), preferred_element_type=F32)
        state_ref[:, gs] = st * chunk_decay[:, gs] + jnp.dot(
            bgT, x_w[:, gs].astype(BF16), preferred_element_type=F32)
        xdt_g = xdt[:, gs]
        w_parts, x_parts = [], []
        for r in range(R):
            hd = g * R + r
            seg = cs2[:, hd:hd + 1] - cs2T[hd:hd + 1, :]
            lmat = jnp.exp2(jnp.where(tri, seg, -jnp.inf))
            w_parts.append((cb * lmat).astype(BF16))
            in_head = jnp.logical_and(lane >= r * SSM_HEAD_DIM, lane < (r + 1) * SSM_HEAD_DIM)
            x_parts.append(jnp.where(in_head, xdt_g, 0.0).astype(BF16))
        y_diag = jnp.dot(jnp.concatenate(w_parts, axis=1), jnp.concatenate(x_parts, axis=0),
                         preferred_element_type=F32)
        y = y_diag + y_off * in_decay[:, gs] + dexp_ref[:, gs] * xs[:, gs]
        y = y * _silu(z_ref[:, gs].astype(F32))
        ms = jnp.mean(y * y, axis=-1, keepdims=True)
        o_ref[:, gs] = (y * lax.rsqrt(ms + NORM_EPS) * nw_ref[:, gs]).astype(o_ref.dtype)


def _ssd(proj, hnorm, w_in, layer, conv_w, conv_b, dt_bias, a_log, d_skip, norm_w, batch, seq):
    nc = seq // CHUNK
    t = batch * seq
    d = hnorm.shape[1]
    H = SSM_HEADS
    assert COL_DT % LANES == 0 and H <= LANES
    pad = lambda v: jnp.pad(v.reshape(1, H), ((0, 0), (0, LANES - H)))
    expand = np.zeros((LANES, D_BRANCH), np.float32)
    for hd in range(H):
        expand[hd, hd * SSM_HEAD_DIM:(hd + 1) * SSM_HEAD_DIM] = 1.0
    d_exp = jnp.repeat(d_skip, SSM_HEAD_DIM).reshape(1, D_BRANCH)
    const = lambda shape: pl.BlockSpec(shape, lambda b, c: (0,) * len(shape))
    blk = lambda col0: pl.BlockSpec((CHUNK, D_BRANCH), lambda b, c: (b * nc + c, col0 // D_BRANCH))
    return pl.pallas_call(
        _ssd_kernel,
        grid=(batch, nc),
        in_specs=[blk(COL_Z), blk(COL_XS), blk(COL_BC),
                  pl.BlockSpec((CHUNK, d), lambda b, c: (b * nc + c, 0)),
                  pl.BlockSpec((None, LANES, d), lambda b, c: (layer, COL_DT // LANES, 0)),
                  const((CONV_WIDTH, SSM_CONV_DIM)), const((1, SSM_CONV_DIM)),
                  const((1, LANES)), const((1, LANES)),
                  const((1, D_BRANCH)), const((1, D_BRANCH)), const((BF16_TERMS * LANES, D_BRANCH))],
        out_specs=pl.BlockSpec((CHUNK, D_BRANCH), lambda b, c: (b * nc + c, 0)),
        out_shape=jax.ShapeDtypeStruct((t, D_BRANCH), BF16),
        scratch_shapes=[pltpu.VMEM((2 * SUBLANES, SSM_CONV_DIM), F32),
                        pltpu.VMEM((SSM_STATE, D_BRANCH), F32),
                        pltpu.VMEM((LANES, d), BF16)],
        compiler_params=_params("parallel", "arbitrary"),
        name="ssd",
    )(proj, proj, proj, hnorm, w_in, conv_w, conv_b.reshape(1, -1),
      pad(dt_bias), pad(a_log),
      d_exp, norm_w.reshape(1, D_BRANCH), jnp.asarray(np.tile(expand, (BF16_TERMS, 1)), BF16))


def _lru_kernel(x_ref, g_ref, cw_ref, cb_ref, wa_ref, ba_ref, wx_ref, bx_ref, lam_ref,
                o_ref, buf_ref, carry_ref):
    c = pl.program_id(1)
    n = x_ref.shape[0]

    @pl.when(c == 0)
    def _():
        carry_ref[...] = jnp.zeros_like(carry_ref)

    xc = _causal_conv(buf_ref, [x_ref[...]], cw_ref, cb_ref, c == 0)
    ra, ix = [], []
    for blk in range(LRU_BLOCKS):
        xb = xc[:, blk * LRU_BLOCK_DIM:(blk + 1) * LRU_BLOCK_DIM].astype(BF16)
        ra.append(jnp.dot(xb, wa_ref[blk], preferred_element_type=F32))
        ix.append(jnp.dot(xb, wx_ref[blk], preferred_element_type=F32))
    r = _sigmoid(jnp.concatenate(ra, axis=1) + ba_ref[...])
    i = _sigmoid(jnp.concatenate(ix, axis=1) + bx_ref[...])
    log_a = -LRU_C * r * _softplus(-lam_ref[...])
    a = jnp.exp(log_a)
    u = jnp.sqrt(-jnp.tanh(log_a) * (1.0 + a * a)) * (i * xc)

    width = a.shape[1]
    a = a.reshape(n // SUBLANES, SUBLANES, width)
    u = u.reshape(n // SUBLANES, SUBLANES, width)
    sub = lax.broadcasted_iota(jnp.int32, (1, SUBLANES, width), 1)
    d = 1
    while d < SUBLANES:
        keep = sub >= d
        a_sh = jnp.where(keep, pltpu.roll(a, d, 1), 1.0)
        u_sh = jnp.where(keep, pltpu.roll(u, d, 1), 0.0)
        u = a * u_sh + u
        a = a * a_sh
        d *= 2
    h = carry_ref[...]
    groups = []
    for r in range(n // SUBLANES):
        groups.append(u[r] + a[r] * h)
        h = groups[-1][SUBLANES - 1:SUBLANES, :]
    carry_ref[...] = h
    hseq = jnp.concatenate(groups, axis=0)
    o_ref[...] = (hseq * _silu(g_ref[...].astype(F32))).astype(o_ref.dtype)


def _lru(proj, conv_w, conv_b, w_a, b_a, w_x, b_x, lam, batch, seq):
    lc = CHUNK
    nc = seq // lc
    t = batch * seq
    const = lambda shape: pl.BlockSpec(shape, lambda b, c: (0,) * len(shape))
    wshape = (LRU_BLOCKS, LRU_BLOCK_DIM, LRU_BLOCK_DIM)
    return pl.pallas_call(
        _lru_kernel,
        grid=(batch, nc),
        in_specs=[pl.BlockSpec((lc, D_BRANCH), lambda b, c: (b * nc + c, 0)),
                  pl.BlockSpec((lc, D_BRANCH), lambda b, c: (b * nc + c, 1)),
                  const((CONV_WIDTH, D_BRANCH)), const((1, D_BRANCH)),
                  const(wshape), const((1, D_BRANCH)), const(wshape), const((1, D_BRANCH)),
                  const((1, D_BRANCH))],
        out_specs=pl.BlockSpec((lc, D_BRANCH), lambda b, c: (b * nc + c, 0)),
        out_shape=jax.ShapeDtypeStruct((t, D_BRANCH), BF16),
        scratch_shapes=[pltpu.VMEM((2 * SUBLANES, D_BRANCH), F32),
                        pltpu.VMEM((1, D_BRANCH), F32)],
        compiler_params=_params("parallel", "arbitrary"),
        name="rglru",
    )(proj, proj, conv_w, conv_b.reshape(1, -1), w_a.astype(BF16), b_a.reshape(1, -1),
      w_x.astype(BF16), b_x.reshape(1, -1), lam.reshape(1, -1))


def _outproj_kernel(y0_ref, y1_ref, y2_ref, y3_ref, w_ref, x_ref, o_ref, wb_ref):
    @pl.when(pl.program_id(1) == 0)
    def _():
        wb_ref[...] = w_ref[...].astype(BF16)

    y = jnp.concatenate([y0_ref[...], y1_ref[...], y2_ref[...], y3_ref[...]], axis=1)
    o_ref[...] = x_ref[...] + jnp.dot(y, wb_ref[...], preferred_element_type=F32)


def _outproj(ys, w, layer, x):
    t, d = x.shape
    k = w.shape[1]
    tm = min(1024, t)
    tn = 512
    yspec = pl.BlockSpec((tm, D_BRANCH), lambda j, i: (i, 0))
    return pl.pallas_call(
        _outproj_kernel,
        grid=(d // tn, t // tm),
        in_specs=[yspec, yspec, yspec, yspec,
                  pl.BlockSpec((None, k, tn), lambda j, i: (layer, 0, j)),
                  pl.BlockSpec((tm, tn), lambda j, i: (i, j))],
        out_specs=pl.BlockSpec((tm, tn), lambda j, i: (i, j)),
        out_shape=jax.ShapeDtypeStruct((t, d), F32),
        scratch_shapes=[pltpu.VMEM((k, tn), BF16)],
        compiler_params=_params("parallel", "arbitrary"),
        name="outproj",
    )(*ys, w, x)


def kernel(x, norm_w, w_in, ret_gn_w, ssm_conv_w, ssm_conv_b, ssm_dt_bias, ssm_a_log, ssm_d, ssm_norm_w, lru_conv_w, lru_conv_b, lru_w_a, lru_b_a, lru_w_x, lru_b_x, lru_lambda, w_out, final_norm_w):
    batch, seq, d = x.shape
    depth = w_in.shape[0]
    assert seq % CHUNK == 0 and d == 4 * D_BRANCH
    t = batch * seq
    xf = x.reshape(t, d)
    w_in_t = jnp.swapaxes(w_in, 1, 2)
    for l in range(depth):
        hn = _rmsnorm(xf, norm_w[l], BF16)
        proj = _inproj(hn, w_in_t, l, 0, D_MAIN)
        proj_lru = _inproj(hn, w_in_t, l, COL_LRU, 2 * D_BRANCH)
        y_ret = _retention(proj, ret_gn_w[l], batch, seq)
        y_moba = _moba(proj, batch, seq)
        y_ssm = _ssd(proj, hn, w_in_t, l, ssm_conv_w[l], ssm_conv_b[l],
                     ssm_dt_bias[l], ssm_a_log[l], ssm_d[l], ssm_norm_w[l], batch, seq)
        y_lru = _lru(proj_lru, lru_conv_w[l], lru_conv_b[l], lru_w_a[l], lru_b_a[l], lru_w_x[l],
                     lru_b_x[l], lru_lambda[l], batch, seq)
        xf = _outproj((y_ret, y_moba, y_ssm, y_lru), w_out, l, xf)
    return _rmsnorm(xf, final_norm_w, F32).reshape(batch, seq, d)
```

```python
import functools
import math

import jax
import jax.numpy as jnp
import numpy as np
from jax import lax
from jax.experimental import pallas as pl
from jax.experimental.pallas import tpu as pltpu

F32 = jnp.float32
BF16 = jnp.bfloat16

NORM_EPS = 1e-6
LOG2E = math.log2(math.e)
D_BRANCH = 1024
CHUNK = 256
RET_HEADS = 4
RET_HEAD_DIM = 256
MOBA_HEADS = 8
MOBA_HEAD_DIM = 128
MOBA_TOPK = 3
MOBA_HEADS_PER_STEP = 8
SSM_HEADS = 16
SSM_HEAD_DIM = 64
SSM_GROUPS = 4
SSM_STATE = 128
SSM_CONV_DIM = 2048
CONV_WIDTH = 4
LRU_BLOCKS = 8
LRU_BLOCK_DIM = 128
LRU_C = 8.0
LANES = 128
SUBLANES = 8
VMEM_LIMIT = 56 * 1024 * 1024

COL_RET = 0
COL_MOBA = 4 * D_BRANCH
COL_Z = 8 * D_BRANCH
COL_XS = COL_Z + D_BRANCH
COL_BC = COL_XS + D_BRANCH
COL_DT = COL_XS + SSM_CONV_DIM
COL_LRU = COL_DT + SSM_HEADS
D_MAIN = COL_DT

NT_DIMS = (((1,), (1,)), ((), ()))


def _params(*sem):
    return pltpu.CompilerParams(dimension_semantics=sem, vmem_limit_bytes=VMEM_LIMIT)


def _sigmoid(x):
    return 1.0 / (1.0 + jnp.exp(-x))


def _silu(x):
    return x * _sigmoid(x)


def _softplus(x):
    return jnp.maximum(x, 0.0) + jnp.log1p(jnp.exp(-jnp.abs(x)))


def _rmsnorm_kernel(x_ref, w_ref, o_ref):
    x = x_ref[...]
    ms = jnp.mean(x * x, axis=-1, keepdims=True)
    o_ref[...] = (x * lax.rsqrt(ms + NORM_EPS) * w_ref[...]).astype(o_ref.dtype)


def _rmsnorm(x, w, out_dtype):
    t, d = x.shape
    tm = min(512, t)
    return pl.pallas_call(
        _rmsnorm_kernel,
        grid=(t // tm,),
        in_specs=[pl.BlockSpec((tm, d), lambda i: (i, 0)),
                  pl.BlockSpec((1, d), lambda i: (0, 0))],
        out_specs=pl.BlockSpec((tm, d), lambda i: (i, 0)),
        out_shape=jax.ShapeDtypeStruct((t, d), out_dtype),
        compiler_params=_params("parallel"),
        name="rmsnorm",
    )(x, w.reshape(1, d))


def _inproj_kernel(a_ref, wt_ref, o_ref, wb_ref):
    @pl.when(pl.program_id(1) == 0)
    def _():
        wb_ref[...] = wt_ref[...].astype(BF16)

    o_ref[...] = lax.dot_general(a_ref[...], wb_ref[...], NT_DIMS,
                                 preferred_element_type=F32).astype(o_ref.dtype)


def _inproj(h, wt, layer, row0, n):
    t, k = h.shape
    tm = min(512, t)
    tn = 1024
    assert n % tn == 0 and row0 % SUBLANES == 0
    if row0 % tn == 0:
        wspec = pl.BlockSpec((None, tn, k), lambda j, i: (layer, row0 // tn + j, 0))
    else:
        wspec = pl.BlockSpec((None, pl.Element(tn), pl.Element(k)),
                             lambda j, i: (layer, pl.multiple_of(row0 + tn * j, SUBLANES), 0))
    return pl.pallas_call(
        _inproj_kernel,
        grid=(n // tn, t // tm),
        in_specs=[pl.BlockSpec((tm, k), lambda j, i: (i, 0)), wspec],
        out_specs=pl.BlockSpec((tm, tn), lambda j, i: (i, j)),
        out_shape=jax.ShapeDtypeStruct((t, n), BF16),
        scratch_shapes=[pltpu.VMEM((tn, k), BF16)],
        compiler_params=_params("parallel", "arbitrary"),
        name="inproj",
    )(h, wt)


def _retention_chunk(q_ref, k_ref, v_ref, g_ref, gn_ref, o_ref, state_ref):
    C, dh = CHUNK, RET_HEAD_DIM
    row = lax.broadcasted_iota(jnp.int32, (C, C), 0)
    col = lax.broadcasted_iota(jnp.int32, (C, C), 1)
    diff = (row - col).astype(F32)
    rowf = row.astype(F32)
    causal = row >= col
    for h in range(RET_HEADS):
        log_g = math.log1p(-(2.0 ** (-5.0 - h)))
        lg2 = log_g * LOG2E
        sl = slice(h * dh, (h + 1) * dh)
        dmat = jnp.where(causal, jnp.exp2(lg2 * jnp.where(causal, diff, 0.0)), 0.0)
        q_decay = jnp.exp2(lg2 * (rowf + 1.0))
        k_decay = jnp.exp2(lg2 * (C - 1.0 - rowf))
        qb = q_ref[:, sl]
        k = k_ref[:, sl].astype(F32) * (dh ** -0.5)
        kb = k.astype(BF16)
        vb = v_ref[:, sl]
        state = state_ref[h]
        inner = lax.dot_general(qb, kb, NT_DIMS, preferred_element_type=F32) * dmat
        y = jnp.dot(inner.astype(BF16), vb, preferred_element_type=F32)
        y = y + jnp.dot(qb, state.astype(BF16), preferred_element_type=F32) * q_decay
        kdT = (k * k_decay).T.astype(BF16)
        state_ref[h] = math.exp(log_g * C) * state + jnp.dot(kdT, vb, preferred_element_type=F32)
        ms = jnp.mean(y * y, axis=-1, keepdims=True)
        y = y * lax.rsqrt(ms + NORM_EPS) * gn_ref[h]
        o_ref[:, sl] = (y * _silu(g_ref[:, sl].astype(F32))).astype(o_ref.dtype)


BF16_TERMS = 3
MOBA_VT_ROWS = MOBA_HEAD_DIM + 16
MOBA_M_INIT = -1e30


def _split_bf16(x):
    terms = []
    for _ in range(BF16_TERMS - 1):
        hi = x.astype(BF16).astype(F32)
        terms.append(hi)
        x = x - hi
    return terms + [x]


def _moba_kernel(slopes_ref, q_ref, k_ref, v_ref, g_ref, o_ref, kmean_ref, kx_ref, vt_ref, sel_ref,
                 s_ref, acc_ref, m_ref):
    hg = pl.program_id(1)
    i = pl.program_id(2)
    L, dh, HP = CHUNK, MOBA_HEAD_DIM, MOBA_HEADS_PER_STEP
    seq = k_ref.shape[0]
    nb = seq // L
    nbp = kmean_ref.shape[1]
    neg = -jnp.inf
    lane = lax.broadcasted_iota(jnp.int32, (L, LANES), 1)

    @pl.when(i == 0)
    def _():
        kmean_ref[...] = jnp.zeros_like(kmean_ref)
        vt_ref[:, :, dh:, :] = jnp.ones((HP, nb, MOBA_VT_ROWS - dh, L), BF16)
        within = lax.broadcasted_iota(jnp.int32, (L, LANES), 0).astype(F32)
        for n in range(nb):
            kx = jnp.where(lane < BF16_TERMS, within,
                           jnp.where(lane < 2 * BF16_TERMS, float(n), 0.0))
            kx_ref[n * L:(n + 1) * L, :] = kx.astype(BF16)
            for hh in range(HP):
                hs = slice(hh * dh, (hh + 1) * dh)
                kn = k_ref[n * L:(n + 1) * L, hs].astype(F32)
                kmean_ref[hh, n:n + 1, :] = jnp.mean(kn, axis=0, keepdims=True)
                vn = v_ref[n * L:(n + 1) * L, hs].astype(F32)
                vt_ref[hh, n, 0:dh, :] = vn.T.astype(BF16)

    blk_row = lax.broadcasted_iota(jnp.int32, (nbp, L), 0)
    blk_rowf = blk_row.astype(F32)
    key = lax.broadcasted_iota(jnp.int32, (L, L), 0)
    qry = lax.broadcasted_iota(jnp.int32, (L, L), 1)
    lane1 = lax.broadcasted_iota(jnp.int32, (1, LANES), 1)

    def scores(hh, qa, off):
        hs = slice(hh * dh, (hh + 1) * dh)
        ka = jnp.concatenate([k_ref[pl.ds(off, L), hs], kx_ref[pl.ds(off, L), :]], axis=1)
        return lax.dot_general(ka, qa, NT_DIMS, preferred_element_type=F32)

    qas = []
    for hh in range(HP):
        hs = slice(hh * dh, (hh + 1) * dh)
        qb = q_ref[:, hs]
        q = qb.astype(F32)
        km3 = jnp.concatenate([p.astype(BF16) for p in _split_bf16(kmean_ref[hh])], axis=0)
        gate3 = lax.dot_general(km3, qb, NT_DIMS, preferred_element_type=F32)
        gate = gate3[:nbp] + gate3[nbp:2 * nbp] + gate3[2 * nbp:]
        gm = jnp.where(blk_row < i, gate, neg)
        sel = jnp.zeros(gate.shape, F32)
        for _ in range(MOBA_TOPK):
            mx = jnp.max(gm, axis=0, keepdims=True)
            first = jnp.min(jnp.where(gm == mx, blk_rowf, float(nbp)), axis=0, keepdims=True)
            pick = jnp.logical_and(blk_rowf == first, mx > neg)
            sel = jnp.where(pick, 1.0, sel)
            gm = jnp.where(pick, neg, gm)
        sel_ref[hh] = sel
        slope2 = jnp.full((1, LANES), slopes_ref[hg * HP + hh] * LOG2E, F32)
        qx = jnp.zeros((1, LANES), F32)
        for t, term in enumerate(_split_bf16(slope2) + _split_bf16(slope2 * L)):
            qx = jnp.where(lane1 == t, term, qx)
        qa = jnp.concatenate([(q * (dh ** -0.5 * LOG2E)).astype(BF16),
                              jnp.broadcast_to(qx, (L, LANES)).astype(BF16)], axis=1)
        qas.append(qa)
        m_ref[hh] = jnp.full((1, L), MOBA_M_INIT, F32)
        acc_ref[hh] = jnp.zeros((MOBA_VT_ROWS, L), F32)
        s_ref[0, hh] = scores(hh, qa, 0)
        s_ref[1, hh] = scores(hh, qa, pl.multiple_of(jnp.minimum(i, 1) * L, L))

    def update(hh, blk, s, picked):
        m = m_ref[hh]
        cmax = jnp.max(s, axis=0, keepdims=True)
        if picked is None:
            m_new = jnp.maximum(m, cmax)
            p = jnp.exp2(s - m_new)
        else:
            m_new = jnp.maximum(m, jnp.where(picked, cmax, neg))
            p = jnp.exp2(s - jnp.where(picked, m_new, jnp.inf))
        m_ref[hh] = m_new
        acc_ref[hh] = acc_ref[hh] * jnp.exp2(m - m_new) + jnp.dot(
            vt_ref[hh, blk], p.astype(BF16), preferred_element_type=F32)

    def past_block(hh, blk, slot):
        picked = sel_ref[hh, pl.ds(blk, 1), :] > 0.0
        update(hh, blk, s_ref[slot, hh], picked)

    @pl.loop(0, jnp.right_shift(i, 1))
    def _(pair):
        blk = 2 * pair
        nxt0 = pl.multiple_of((blk + 2) * L, L)
        nxt1 = pl.multiple_of(jnp.minimum(blk + 3, i) * L, L)
        for hh in range(HP):
            past_block(hh, blk, 0)
            s_ref[0, hh] = scores(hh, qas[hh], nxt0)
            past_block(hh, blk + 1, 1)
            s_ref[1, hh] = scores(hh, qas[hh], nxt1)

    odd = jnp.bitwise_and(i, 1) == 1

    @pl.when(odd)
    def _():
        for hh in range(HP):
            past_block(hh, i - 1, 0)

    for hh in range(HP):
        hs = slice(hh * dh, (hh + 1) * dh)
        s_own = jnp.where(odd, s_ref[1, hh], s_ref[0, hh])
        update(hh, i, jnp.where(key <= qry, s_own, neg), None)
        acc = acc_ref[hh]
        out = (acc[:dh, :] / acc[dh:dh + 1, :]).T
        o_ref[:, hs] = (out * _silu(g_ref[:, hs].astype(F32))).astype(o_ref.dtype)


def _moba(proj, batch, seq):
    nb = seq // CHUNK
    t = batch * seq
    H, dh, HP = MOBA_HEADS, MOBA_HEAD_DIM, MOBA_HEADS_PER_STEP
    nbp = -(-nb // SUBLANES) * SUBLANES
    w = HP * dh
    base = COL_MOBA // w
    per = D_BRANCH // w
    slopes = jnp.asarray(np.exp2(-8.0 * (np.arange(H, dtype=np.float64) + 1.0) / H), F32)
    once = pl.Buffered(1)
    return pl.pallas_call(
        _moba_kernel,
        grid=(batch, H // HP, nb),
        in_specs=[pl.BlockSpec(memory_space=pltpu.SMEM),
                  pl.BlockSpec((CHUNK, w), lambda b, h, i: (b * nb + i, base + h)),
                  pl.BlockSpec((seq, w), lambda b, h, i: (b, base + per + h), pipeline_mode=once),
                  pl.BlockSpec((seq, w), lambda b, h, i: (b, base + 2 * per + h), pipeline_mode=once),
                  pl.BlockSpec((CHUNK, w), lambda b, h, i: (b * nb + i, base + 3 * per + h))],
        out_specs=pl.BlockSpec((CHUNK, w), lambda b, h, i: (b * nb + i, h)),
        out_shape=jax.ShapeDtypeStruct((t, D_BRANCH), BF16),
        scratch_shapes=[pltpu.VMEM((HP, nbp, dh), F32),
                        pltpu.VMEM((seq, LANES), BF16),
                        pltpu.VMEM((HP, nb, MOBA_VT_ROWS, CHUNK), BF16),
                        pltpu.VMEM((HP, nbp, CHUNK), F32),
                        pltpu.VMEM((2, HP, CHUNK, CHUNK), F32),
                        pltpu.VMEM((HP, MOBA_VT_ROWS, CHUNK), F32),
                        pltpu.VMEM((HP, 1, CHUNK), F32)],
        compiler_params=_params("parallel", "parallel", "arbitrary"),
        name="moba",
    )(slopes, proj, proj, proj, proj)


def _causal_conv(buf_ref, parts, cw_ref, cb_ref):
    n = parts[0].shape[0]
    x = jnp.concatenate(parts, axis=1)
    xf = x.astype(F32)
    buf_ref[SUBLANES:2 * SUBLANES, :] = xf[0:SUBLANES, :]
    row = lax.broadcasted_iota(jnp.int32, (n, n), 0)
    col = lax.broadcasted_iota(jnp.int32, (n, n), 1)
    y = cb_ref[...] + xf * cw_ref[CONV_WIDTH - 1:CONV_WIDTH, :]
    head = cb_ref[...] + xf[0:SUBLANES, :] * cw_ref[CONV_WIDTH - 1:CONV_WIDTH, :]
    for k in range(CONV_WIDTH - 1):
        back = CONV_WIDTH - 1 - k
        shift = jnp.where(row - col == back, 1.0, 0.0).astype(BF16)
        y = y + jnp.dot(shift, x, preferred_element_type=F32) * cw_ref[k:k + 1, :]
        head = head + buf_ref[SUBLANES - back:2 * SUBLANES - back, :] * cw_ref[k:k + 1, :]
    buf_ref[0:SUBLANES, :] = xf[n - SUBLANES:n, :]
    return jnp.concatenate([head, y[SUBLANES:, :]], axis=0)


def _ssd_chunk(z_ref, xs_ref, bc_ref, h_ref, cw_ref, cb_ref, dtb_ref, alog_ref,
               dexp_ref, nw_ref, e_ref, o_ref, buf_ref, state_ref, wdtb_ref):
    Q, N, G = CHUNK, SSM_STATE, SSM_GROUPS
    R = SSM_HEADS // G
    GW = D_BRANCH // G

    act = _silu(_causal_conv(buf_ref, [xs_ref[...], bc_ref[...]], cw_ref, cb_ref))
    xs = act[:, :D_BRANCH]
    bm = act[:, D_BRANCH:D_BRANCH + G * N]
    cm = act[:, D_BRANCH + G * N:]

    head_lane = lax.broadcasted_iota(jnp.int32, (Q, LANES), 1) < SSM_HEADS
    dt_raw = lax.dot_general(h_ref[...], wdtb_ref[...], NT_DIMS, preferred_element_type=F32)
    dt = jnp.where(head_lane, _softplus(dt_raw + dtb_ref[...]), 0.0)
    da = dt * (-jnp.exp(alog_ref[...]))

    row = lax.broadcasted_iota(jnp.int32, (Q, Q), 0)
    col = lax.broadcasted_iota(jnp.int32, (Q, Q), 1)
    tri = row >= col
    lower = jnp.where(tri, 1.0, 0.0).astype(BF16)
    cs3 = jnp.dot(lower, jnp.concatenate([p.astype(BF16) for p in _split_bf16(da)], axis=1),
                  preferred_element_type=F32)
    da_cs = cs3[:, :LANES] + cs3[:, LANES:2 * LANES] + cs3[:, 2 * LANES:]
    cs2 = da_cs * LOG2E
    cs2T = cs2.T

    def expand(v):
        lhs = jnp.concatenate([p.astype(BF16) for p in _split_bf16(v)], axis=1)
        return jnp.dot(lhs, e_ref[...], preferred_element_type=F32)

    dt_exp = expand(dt)
    cs_exp = expand(cs2)
    last_exp = cs_exp[Q - 1:Q, :]
    xdt = xs * dt_exp
    x_w = xdt * jnp.exp2(last_exp - cs_exp)
    chunk_decay = jnp.exp2(last_exp)
    in_decay = jnp.exp2(cs_exp)

    lane = lax.broadcasted_iota(jnp.int32, (Q, GW), 1)
    for g in range(G):
        gs = slice(g * GW, (g + 1) * GW)
        bg = bm[:, g * N:(g + 1) * N]
        cgb = cm[:, g * N:(g + 1) * N].astype(BF16)
        bgT = bg.T.astype(BF16)
        cb = jnp.dot(cgb, bgT, preferred_element_type=F32)
        st = state_ref[:, gs]
        y_off = jnp.dot(cgb, st.astype(BF16), preferred_element_type=F32)
        state_ref[:, gs] = st * chunk_decay[:, gs] + jnp.dot(
            bgT, x_w[:, gs].astype(BF16), preferred_element_type=F32)
        xdt_g = xdt[:, gs]
        w_parts, x_parts = [], []
        for r in range(R):
            hd = g * R + r
            seg = cs2[:, hd:hd + 1] - cs2T[hd:hd + 1, :]
            lmat = jnp.exp2(jnp.where(tri, seg, -jnp.inf))
            w_parts.append((cb * lmat).astype(BF16))
            in_head = jnp.logical_and(lane >= r * SSM_HEAD_DIM, lane < (r + 1) * SSM_HEAD_DIM)
            x_parts.append(jnp.where(in_head, xdt_g, 0.0).astype(BF16))
        y_diag = jnp.dot(jnp.concatenate(w_parts, axis=1), jnp.concatenate(x_parts, axis=0),
                         preferred_element_type=F32)
        y = y_diag + y_off * in_decay[:, gs] + dexp_ref[:, gs] * xs[:, gs]
        y = y * _silu(z_ref[:, gs].astype(F32))
        ms = jnp.mean(y * y, axis=-1, keepdims=True)
        o_ref[:, gs] = (y * lax.rsqrt(ms + NORM_EPS) * nw_ref[:, gs]).astype(o_ref.dtype)


def _lru_chunk(x_ref, g_ref, cw_ref, cb_ref, wa_ref, ba_ref, wx_ref, bx_ref, lam_ref,
               o_ref, buf_ref, carry_ref):
    n = x_ref.shape[0]
    xc = _causal_conv(buf_ref, [x_ref[...]], cw_ref, cb_ref)
    ra, ix = [], []
    for blk in range(LRU_BLOCKS):
        xb = xc[:, blk * LRU_BLOCK_DIM:(blk + 1) * LRU_BLOCK_DIM].astype(BF16)
        ra.append(jnp.dot(xb, wa_ref[blk], preferred_element_type=F32))
        ix.append(jnp.dot(xb, wx_ref[blk], preferred_element_type=F32))
    r = _sigmoid(jnp.concatenate(ra, axis=1) + ba_ref[...])
    i = _sigmoid(jnp.concatenate(ix, axis=1) + bx_ref[...])
    log_a = -LRU_C * r * _softplus(-lam_ref[...])
    a = jnp.exp(log_a)
    u = jnp.sqrt(-jnp.tanh(log_a) * (1.0 + a * a)) * (i * xc)

    width = a.shape[1]
    a = a.reshape(n // SUBLANES, SUBLANES, width)
    u = u.reshape(n // SUBLANES, SUBLANES, width)
    sub = lax.broadcasted_iota(jnp.int32, (1, SUBLANES, width), 1)
    d = 1
    while d < SUBLANES:
        keep = sub >= d
        a_sh = jnp.where(keep, pltpu.roll(a, d, 1), 1.0)
        u_sh = jnp.where(keep, pltpu.roll(u, d, 1), 0.0)
        u = a * u_sh + u
        a = a * a_sh
        d *= 2
    h = carry_ref[...]
    groups = []
    for r in range(n // SUBLANES):
        groups.append(u[r] + a[r] * h)
        h = groups[-1][SUBLANES - 1:SUBLANES, :]
    carry_ref[...] = h
    hseq = jnp.concatenate(groups, axis=0)
    o_ref[...] = (hseq * _silu(g_ref[...].astype(F32))).astype(o_ref.dtype)


N_RET_IN, N_SSD_IN, N_LRU_IN = 5, 12, 9


def _mixers_kernel(*refs):
    ret_in = refs[:N_RET_IN]
    ssd_in = refs[N_RET_IN:N_RET_IN + N_SSD_IN]
    lru_in = refs[N_RET_IN + N_SSD_IN:N_RET_IN + N_SSD_IN + N_LRU_IN]
    (ret_out, ssd_out, lru_out, ret_state, ssd_buf, ssd_state, wdtb_ref, lru_buf,
     lru_carry) = refs[N_RET_IN + N_SSD_IN + N_LRU_IN:]
    z_ref, xs_ref, bc_ref, h_ref, wdt_ref = ssd_in[:5]

    @pl.when(pl.program_id(1) == 0)
    def _():
        ret_state[...] = jnp.zeros_like(ret_state)
        ssd_state[...] = jnp.zeros_like(ssd_state)
        lru_carry[...] = jnp.zeros_like(lru_carry)
        ssd_buf[0:SUBLANES, :] = jnp.zeros((SUBLANES, ssd_buf.shape[1]), F32)
        lru_buf[0:SUBLANES, :] = jnp.zeros((SUBLANES, lru_buf.shape[1]), F32)
        wdtb_ref[...] = wdt_ref[...].astype(BF16)

    _retention_chunk(*ret_in, ret_out, ret_state)
    _ssd_chunk(z_ref, xs_ref, bc_ref, h_ref, *ssd_in[5:], ssd_out, ssd_buf, ssd_state, wdtb_ref)
    _lru_chunk(*lru_in, lru_out, lru_buf, lru_carry)


def _mixers(proj, proj_lru, hnorm, w_in_t, layer, ret_gn_w, ssm_conv_w, ssm_conv_b, ssm_dt_bias,
            ssm_a_log, ssm_d, ssm_norm_w, lru_conv_w, lru_conv_b, lru_w_a, lru_b_a, lru_w_x,
            lru_b_x, lru_lambda, batch, seq):
    nc = seq // CHUNK
    t = batch * seq
    d = hnorm.shape[1]
    H = SSM_HEADS
    assert COL_DT % LANES == 0 and H <= LANES
    pad = lambda v: jnp.pad(v.reshape(1, H), ((0, 0), (0, LANES - H)))
    expand = np.zeros((LANES, D_BRANCH), np.float32)
    for hd in range(H):
        expand[hd, hd * SSM_HEAD_DIM:(hd + 1) * SSM_HEAD_DIM] = 1.0
    d_exp = jnp.repeat(ssm_d, SSM_HEAD_DIM).reshape(1, D_BRANCH)
    const = lambda shape: pl.BlockSpec(shape, lambda b, c: (0,) * len(shape))
    blk = lambda col0: pl.BlockSpec((CHUNK, D_BRANCH), lambda b, c: (b * nc + c, col0 // D_BRANCH))
    wshape = (LRU_BLOCKS, LRU_BLOCK_DIM, LRU_BLOCK_DIM)
    ret_specs = [blk(COL_RET + j * D_BRANCH) for j in range(4)] + [
        pl.BlockSpec((RET_HEADS, 1, RET_HEAD_DIM), lambda b, c: (0, 0, 0))]
    ssd_specs = [blk(COL_Z), blk(COL_XS), blk(COL_BC),
                 pl.BlockSpec((CHUNK, d), lambda b, c: (b * nc + c, 0)),
                 pl.BlockSpec((None, LANES, d), lambda b, c: (layer, COL_DT // LANES, 0)),
                 const((CONV_WIDTH, SSM_CONV_DIM)), const((1, SSM_CONV_DIM)),
                 const((1, LANES)), const((1, LANES)),
                 const((1, D_BRANCH)), const((1, D_BRANCH)),
                 const((BF16_TERMS * LANES, D_BRANCH))]
    lru_specs = [blk(0), blk(D_BRANCH),
                 const((CONV_WIDTH, D_BRANCH)), const((1, D_BRANCH)),
                 const(wshape), const((1, D_BRANCH)), const(wshape), const((1, D_BRANCH)),
                 const((1, D_BRANCH))]
    assert (len(ret_specs), len(ssd_specs), len(lru_specs)) == (N_RET_IN, N_SSD_IN, N_LRU_IN)
    out_spec = pl.BlockSpec((CHUNK, D_BRANCH), lambda b, c: (b * nc + c, 0))
    out_shape = jax.ShapeDtypeStruct((t, D_BRANCH), BF16)
    return pl.pallas_call(
        _mixers_kernel,
        grid=(batch, nc),
        in_specs=ret_specs + ssd_specs + lru_specs,
        out_specs=[out_spec] * 3,
        out_shape=[out_shape] * 3,
        scratch_shapes=[pltpu.VMEM((RET_HEADS, RET_HEAD_DIM, RET_HEAD_DIM), F32),
                        pltpu.VMEM((2 * SUBLANES, SSM_CONV_DIM), F32),
                        pltpu.VMEM((SSM_STATE, D_BRANCH), F32),
                        pltpu.VMEM((LANES, d), BF16),
                        pltpu.VMEM((2 * SUBLANES, D_BRANCH), F32),
                        pltpu.VMEM((1, D_BRANCH), F32)],
        compiler_params=_params("parallel", "arbitrary"),
        name="mixers",
    )(proj, proj, proj, proj, ret_gn_w.reshape(RET_HEADS, 1, RET_HEAD_DIM),
      proj, proj, proj, hnorm, w_in_t, ssm_conv_w, ssm_conv_b.reshape(1, -1),
      pad(ssm_dt_bias), pad(ssm_a_log), d_exp, ssm_norm_w.reshape(1, D_BRANCH),
      jnp.asarray(np.tile(expand, (BF16_TERMS, 1)), BF16),
      proj_lru, proj_lru, lru_conv_w, lru_conv_b.reshape(1, -1), lru_w_a.astype(BF16),
      lru_b_a.reshape(1, -1), lru_w_x.astype(BF16), lru_b_x.reshape(1, -1),
      lru_lambda.reshape(1, -1))


def _outproj_kernel(y0_ref, y1_ref, y2_ref, y3_ref, w_ref, x_ref, o_ref, wb_ref):
    @pl.when(pl.program_id(1) == 0)
    def _():
        wb_ref[...] = w_ref[...].astype(BF16)

    y = jnp.concatenate([y0_ref[...], y1_ref[...], y2_ref[...], y3_ref[...]], axis=1)
    o_ref[...] = x_ref[...] + jnp.dot(y, wb_ref[...], preferred_element_type=F32)


def _outproj(ys, w, layer, x):
    t, d = x.shape
    k = w.shape[1]
    tm = min(1024, t)
    tn = 512
    yspec = pl.BlockSpec((tm, D_BRANCH), lambda j, i: (i, 0))
    return pl.pallas_call(
        _outproj_kernel,
        grid=(d // tn, t // tm),
        in_specs=[yspec, yspec, yspec, yspec,
                  pl.BlockSpec((None, k, tn), lambda j, i: (layer, 0, j)),
                  pl.BlockSpec((tm, tn), lambda j, i: (i, j))],
        out_specs=pl.BlockSpec((tm, tn), lambda j, i: (i, j)),
        out_shape=jax.ShapeDtypeStruct((t, d), F32),
        scratch_shapes=[pltpu.VMEM((k, tn), BF16)],
        compiler_params=_params("parallel", "arbitrary"),
        name="outproj",
    )(*ys, w, x)


def kernel(x, norm_w, w_in, ret_gn_w, ssm_conv_w, ssm_conv_b, ssm_dt_bias, ssm_a_log, ssm_d, ssm_norm_w, lru_conv_w, lru_conv_b, lru_w_a, lru_b_a, lru_w_x, lru_b_x, lru_lambda, w_out, final_norm_w):
    batch, seq, d = x.shape
    depth = w_in.shape[0]
    assert seq % CHUNK == 0 and d == 4 * D_BRANCH
    t = batch * seq
    xf = x.reshape(t, d)
    w_in_t = jnp.swapaxes(w_in, 1, 2)
    for l in range(depth):
        hn = _rmsnorm(xf, norm_w[l], BF16)
        proj = _inproj(hn, w_in_t, l, 0, D_MAIN)
        proj_lru = _inproj(hn, w_in_t, l, COL_LRU, 2 * D_BRANCH)
        y_moba = _moba(proj, batch, seq)
        y_ret, y_ssm, y_lru = _mixers(
            proj, proj_lru, hn, w_in_t, l, ret_gn_w[l], ssm_conv_w[l], ssm_conv_b[l],
            ssm_dt_bias[l], ssm_a_log[l], ssm_d[l], ssm_norm_w[l], lru_conv_w[l], lru_conv_b[l],
            lru_w_a[l], lru_b_a[l], lru_w_x[l], lru_b_x[l], lru_lambda[l], batch, seq)
        xf = _outproj((y_ret, y_moba, y_ssm, y_lru), w_out, l, xf)
    return _rmsnorm(xf, final_norm_w, F32).reshape(batch, seq, d)
```

```python
import functools
import math

import jax
import jax.numpy as jnp
import numpy as np
from jax import lax
from jax.experimental import pallas as pl
from jax.experimental.pallas import tpu as pltpu

F32 = jnp.float32
BF16 = jnp.bfloat16

NORM_EPS = 1e-6
LOG2E = math.log2(math.e)
D_BRANCH = 1024
CHUNK = 256
RET_HEADS = 4
RET_HEAD_DIM = 256
MOBA_HEADS = 8
MOBA_HEAD_DIM = 128
MOBA_TOPK = 3
MOBA_HEADS_PER_STEP = 8
SSM_HEADS = 16
SSM_HEAD_DIM = 64
SSM_GROUPS = 4
SSM_STATE = 128
SSM_CONV_DIM = 2048
CONV_WIDTH = 4
LRU_BLOCKS = 8
LRU_BLOCK_DIM = 128
LRU_C = 8.0
LANES = 128
SUBLANES = 8
VMEM_LIMIT = 61 * 1024 * 1024

COL_RET = 0
COL_MOBA = 4 * D_BRANCH
COL_Z = 8 * D_BRANCH
COL_XS = COL_Z + D_BRANCH
COL_BC = COL_XS + D_BRANCH
COL_DT = COL_XS + SSM_CONV_DIM
COL_LRU = COL_DT + SSM_HEADS
D_MAIN = COL_DT

NT_DIMS = (((1,), (1,)), ((), ()))


def _params(*sem):
    return pltpu.CompilerParams(dimension_semantics=sem, vmem_limit_bytes=VMEM_LIMIT)


def _sigmoid(x):
    return 1.0 / (1.0 + jnp.exp(-x))


def _silu(x):
    return x * _sigmoid(x)


def _softplus(x):
    return jnp.maximum(x, 0.0) + jnp.log1p(jnp.exp(-jnp.abs(x)))


def _rmsnorm_kernel(x_ref, w_ref, o_ref):
    x = x_ref[...]
    ms = jnp.mean(x * x, axis=-1, keepdims=True)
    o_ref[...] = (x * lax.rsqrt(ms + NORM_EPS) * w_ref[...]).astype(o_ref.dtype)


def _rmsnorm(x, w, out_dtype):
    t, d = x.shape
    tm = min(512, t)
    return pl.pallas_call(
        _rmsnorm_kernel,
        grid=(t // tm,),
        in_specs=[pl.BlockSpec((tm, d), lambda i: (i, 0)),
                  pl.BlockSpec((1, d), lambda i: (0, 0))],
        out_specs=pl.BlockSpec((tm, d), lambda i: (i, 0)),
        out_shape=jax.ShapeDtypeStruct((t, d), out_dtype),
        compiler_params=_params("parallel"),
        name="rmsnorm",
    )(x, w.reshape(1, d))


def _inproj_kernel(a_ref, wt_ref, o_ref, wb_ref):
    @pl.when(pl.program_id(1) == 0)
    def _():
        wb_ref[...] = wt_ref[...].astype(BF16)

    o_ref[...] = lax.dot_general(a_ref[...], wb_ref[...], NT_DIMS,
                                 preferred_element_type=F32).astype(o_ref.dtype)


def _inproj(h, wt, layer, row0, n):
    t, k = h.shape
    tm = min(512, t)
    tn = 1024
    assert n % tn == 0 and row0 % SUBLANES == 0
    if row0 % tn == 0:
        wspec = pl.BlockSpec((None, tn, k), lambda j, i: (layer, row0 // tn + j, 0))
    else:
        wspec = pl.BlockSpec((None, pl.Element(tn), pl.Element(k)),
                             lambda j, i: (layer, pl.multiple_of(row0 + tn * j, SUBLANES), 0))
    return pl.pallas_call(
        _inproj_kernel,
        grid=(n // tn, t // tm),
        in_specs=[pl.BlockSpec((tm, k), lambda j, i: (i, 0)), wspec],
        out_specs=pl.BlockSpec((tm, tn), lambda j, i: (i, j)),
        out_shape=jax.ShapeDtypeStruct((t, n), BF16),
        scratch_shapes=[pltpu.VMEM((tn, k), BF16)],
        compiler_params=_params("parallel", "arbitrary"),
        name="inproj",
    )(h, wt)


def _retention_log_decay(h):
    return math.log1p(-(2.0 ** (-5.0 - h)))


def _retention_decays(decay_ref):
    C = CHUNK
    row = lax.broadcasted_iota(jnp.int32, (C, C), 0)
    col = lax.broadcasted_iota(jnp.int32, (C, C), 1)
    diff = (row - col).astype(F32)
    rowf = row.astype(F32)
    causal = row >= col
    for h in range(RET_HEADS):
        lg2 = _retention_log_decay(h) * LOG2E
        decay_ref[h, 0] = jnp.where(causal, jnp.exp2(lg2 * jnp.where(causal, diff, 0.0)), 0.0)
        decay_ref[h, 1] = jnp.exp2(lg2 * (rowf + 1.0))
        decay_ref[h, 2] = jnp.exp2(lg2 * (C - 1.0 - rowf))


def _retention_chunk(q_ref, k_ref, v_ref, g_ref, gn_ref, o_ref, state_ref, decay_ref):
    C, dh = CHUNK, RET_HEAD_DIM
    for h in range(RET_HEADS):
        log_g = _retention_log_decay(h)
        sl = slice(h * dh, (h + 1) * dh)
        dmat, q_decay, k_decay = decay_ref[h, 0], decay_ref[h, 1], decay_ref[h, 2]
        qb = q_ref[:, sl]
        k = k_ref[:, sl].astype(F32) * (dh ** -0.5)
        kb = k.astype(BF16)
        vb = v_ref[:, sl]
        state = state_ref[h]
        inner = lax.dot_general(qb, kb, NT_DIMS, preferred_element_type=F32) * dmat
        y = jnp.dot(inner.astype(BF16), vb, preferred_element_type=F32)
        y = y + jnp.dot(qb, state.astype(BF16), preferred_element_type=F32) * q_decay
        kdT = (k * k_decay).T.astype(BF16)
        state_ref[h] = math.exp(log_g * C) * state + jnp.dot(kdT, vb, preferred_element_type=F32)
        ms = jnp.mean(y * y, axis=-1, keepdims=True)
        y = y * lax.rsqrt(ms + NORM_EPS) * gn_ref[h]
        o_ref[:, sl] = (y * _silu(g_ref[:, sl].astype(F32))).astype(o_ref.dtype)


BF16_TERMS = 3
MOBA_VT_ROWS = MOBA_HEAD_DIM + 16
MOBA_M_INIT = -1e30


def _split_bf16(x):
    terms = []
    for _ in range(BF16_TERMS - 1):
        hi = x.astype(BF16).astype(F32)
        terms.append(hi)
        x = x - hi
    return terms + [x]


def _moba_kernel(slopes_ref, q_ref, k_ref, v_ref, g_ref, o_ref, kmean_ref, kx_ref, vt_ref, sel_ref,
                 s_ref, acc_ref, m_ref):
    hg = pl.program_id(1)
    i = pl.program_id(2)
    L, dh, HP = CHUNK, MOBA_HEAD_DIM, MOBA_HEADS_PER_STEP
    seq = k_ref.shape[0]
    nb = seq // L
    nbp = kmean_ref.shape[1]
    neg = -jnp.inf
    lane = lax.broadcasted_iota(jnp.int32, (L, LANES), 1)

    @pl.when(i == 0)
    def _():
        kmean_ref[...] = jnp.zeros_like(kmean_ref)
        vt_ref[:, :, dh:, :] = jnp.ones((HP, nb, MOBA_VT_ROWS - dh, L), BF16)
        within = lax.broadcasted_iota(jnp.int32, (L, LANES), 0).astype(F32)
        for n in range(nb):
            kx = jnp.where(lane < BF16_TERMS, within,
                           jnp.where(lane < 2 * BF16_TERMS, float(n), 0.0))
            kx_ref[n * L:(n + 1) * L, :] = kx.astype(BF16)
            for hh in range(HP):
                hs = slice(hh * dh, (hh + 1) * dh)
                kn = k_ref[n * L:(n + 1) * L, hs].astype(F32)
                kmean_ref[hh, n:n + 1, :] = jnp.mean(kn, axis=0, keepdims=True)
                vn = v_ref[n * L:(n + 1) * L, hs].astype(F32)
                vt_ref[hh, n, 0:dh, :] = vn.T.astype(BF16)

    blk_row = lax.broadcasted_iota(jnp.int32, (nbp, L), 0)
    blk_rowf = blk_row.astype(F32)
    key = lax.broadcasted_iota(jnp.int32, (L, L), 0)
    qry = lax.broadcasted_iota(jnp.int32, (L, L), 1)
    lane1 = lax.broadcasted_iota(jnp.int32, (1, LANES), 1)

    def scores(hh, qa, off):
        hs = slice(hh * dh, (hh + 1) * dh)
        ka = jnp.concatenate([k_ref[pl.ds(off, L), hs], kx_ref[pl.ds(off, L), :]], axis=1)
        return lax.dot_general(ka, qa, NT_DIMS, preferred_element_type=F32)

    qas = []
    for hh in range(HP):
        hs = slice(hh * dh, (hh + 1) * dh)
        qb = q_ref[:, hs]
        q = qb.astype(F32)
        km3 = jnp.concatenate([p.astype(BF16) for p in _split_bf16(kmean_ref[hh])], axis=0)
        gate3 = lax.dot_general(km3, qb, NT_DIMS, preferred_element_type=F32)
        gate = gate3[:nbp] + gate3[nbp:2 * nbp] + gate3[2 * nbp:]
        gm = jnp.where(blk_row < i, gate, neg)
        sel = jnp.zeros(gate.shape, F32)
        for _ in range(MOBA_TOPK):
            mx = jnp.max(gm, axis=0, keepdims=True)
            first = jnp.min(jnp.where(gm == mx, blk_rowf, float(nbp)), axis=0, keepdims=True)
            pick = jnp.logical_and(blk_rowf == first, mx > neg)
            sel = jnp.where(pick, 1.0, sel)
            gm = jnp.where(pick, neg, gm)
        sel_ref[hh] = sel
        slope2 = jnp.full((1, LANES), slopes_ref[hg * HP + hh] * LOG2E, F32)
        qx = jnp.zeros((1, LANES), F32)
        for t, term in enumerate(_split_bf16(slope2) + _split_bf16(slope2 * L)):
            qx = jnp.where(lane1 == t, term, qx)
        qa = jnp.concatenate([(q * (dh ** -0.5 * LOG2E)).astype(BF16),
                              jnp.broadcast_to(qx, (L, LANES)).astype(BF16)], axis=1)
        qas.append(qa)
        m_ref[hh] = jnp.full((1, L), MOBA_M_INIT, F32)
        acc_ref[hh] = jnp.zeros((MOBA_VT_ROWS, L), F32)
        s_ref[0, hh] = scores(hh, qa, 0)
        s_ref[1, hh] = scores(hh, qa, pl.multiple_of(jnp.minimum(i, 1) * L, L))

    def update(hh, blk, s, picked):
        m = m_ref[hh]
        cmax = jnp.max(s, axis=0, keepdims=True)
        if picked is None:
            m_new = jnp.maximum(m, cmax)
            p = jnp.exp2(s - m_new)
        else:
            m_new = jnp.maximum(m, jnp.where(picked, cmax, neg))
            p = jnp.exp2(s - jnp.where(picked, m_new, jnp.inf))
        m_ref[hh] = m_new
        acc_ref[hh] = acc_ref[hh] * jnp.exp2(m - m_new) + jnp.dot(
            vt_ref[hh, blk], p.astype(BF16), preferred_element_type=F32)

    def past_block(hh, blk, slot):
        picked = sel_ref[hh, pl.ds(blk, 1), :] > 0.0
        update(hh, blk, s_ref[slot, hh], picked)

    @pl.loop(0, jnp.right_shift(i, 1))
    def _(pair):
        blk = 2 * pair
        nxt0 = pl.multiple_of((blk + 2) * L, L)
        nxt1 = pl.multiple_of(jnp.minimum(blk + 3, i) * L, L)
        for hh in range(HP):
            past_block(hh, blk, 0)
            s_ref[0, hh] = scores(hh, qas[hh], nxt0)
            past_block(hh, blk + 1, 1)
            s_ref[1, hh] = scores(hh, qas[hh], nxt1)

    odd = jnp.bitwise_and(i, 1) == 1

    @pl.when(odd)
    def _():
        for hh in range(HP):
            past_block(hh, i - 1, 0)

    for hh in range(HP):
        hs = slice(hh * dh, (hh + 1) * dh)
        s_own = jnp.where(odd, s_ref[1, hh], s_ref[0, hh])
        update(hh, i, jnp.where(key <= qry, s_own, neg), None)
        acc = acc_ref[hh]
        out = (acc[:dh, :] / acc[dh:dh + 1, :]).T
        o_ref[:, hs] = (out * _silu(g_ref[:, hs].astype(F32))).astype(o_ref.dtype)


def _moba(proj, batch, seq):
    nb = seq // CHUNK
    t = batch * seq
    H, dh, HP = MOBA_HEADS, MOBA_HEAD_DIM, MOBA_HEADS_PER_STEP
    nbp = -(-nb // SUBLANES) * SUBLANES
    w = HP * dh
    base = COL_MOBA // w
    per = D_BRANCH // w
    slopes = jnp.asarray(np.exp2(-8.0 * (np.arange(H, dtype=np.float64) + 1.0) / H), F32)
    once = pl.Buffered(1)
    return pl.pallas_call(
        _moba_kernel,
        grid=(batch, H // HP, nb),
        in_specs=[pl.BlockSpec(memory_space=pltpu.SMEM),
                  pl.BlockSpec((CHUNK, w), lambda b, h, i: (b * nb + i, base + h)),
                  pl.BlockSpec((seq, w), lambda b, h, i: (b, base + per + h), pipeline_mode=once),
                  pl.BlockSpec((seq, w), lambda b, h, i: (b, base + 2 * per + h), pipeline_mode=once),
                  pl.BlockSpec((CHUNK, w), lambda b, h, i: (b * nb + i, base + 3 * per + h))],
        out_specs=pl.BlockSpec((CHUNK, w), lambda b, h, i: (b * nb + i, h)),
        out_shape=jax.ShapeDtypeStruct((t, D_BRANCH), BF16),
        scratch_shapes=[pltpu.VMEM((HP, nbp, dh), F32),
                        pltpu.VMEM((seq, LANES), BF16),
                        pltpu.VMEM((HP, nb, MOBA_VT_ROWS, CHUNK), BF16),
                        pltpu.VMEM((HP, nbp, CHUNK), F32),
                        pltpu.VMEM((2, HP, CHUNK, CHUNK), F32),
                        pltpu.VMEM((HP, MOBA_VT_ROWS, CHUNK), F32),
                        pltpu.VMEM((HP, 1, CHUNK), F32)],
        compiler_params=_params("parallel", "parallel", "arbitrary"),
        name="moba",
    )(slopes, proj, proj, proj, proj)


def _band_matrices(band_ref):
    n = band_ref.shape[1]
    row = lax.broadcasted_iota(jnp.int32, (n, n), 0)
    col = lax.broadcasted_iota(jnp.int32, (n, n), 1)
    for k in range(CONV_WIDTH - 1):
        band_ref[k] = jnp.where(row - col == CONV_WIDTH - 1 - k, 1.0, 0.0).astype(BF16)
    band_ref[CONV_WIDTH - 1] = jnp.where(row >= col, 1.0, 0.0).astype(BF16)


def _causal_conv(buf_ref, parts, cw_ref, cb_ref, band_ref):
    n = parts[0].shape[0]
    x = jnp.concatenate(parts, axis=1)
    xf = x.astype(F32)
    buf_ref[SUBLANES:2 * SUBLANES, :] = xf[0:SUBLANES, :]
    y = cb_ref[...] + xf * cw_ref[CONV_WIDTH - 1:CONV_WIDTH, :]
    head = cb_ref[...] + xf[0:SUBLANES, :] * cw_ref[CONV_WIDTH - 1:CONV_WIDTH, :]
    for k in range(CONV_WIDTH - 1):
        back = CONV_WIDTH - 1 - k
        y = y + jnp.dot(band_ref[k], x, preferred_element_type=F32) * cw_ref[k:k + 1, :]
        head = head + buf_ref[SUBLANES - back:2 * SUBLANES - back, :] * cw_ref[k:k + 1, :]
    buf_ref[0:SUBLANES, :] = xf[n - SUBLANES:n, :]
    return jnp.concatenate([head, y[SUBLANES:, :]], axis=0)


def _ssd_chunk(z_ref, xs_ref, bc_ref, h_ref, cw_ref, cb_ref, dtb_ref, alog_ref,
               dexp_ref, nw_ref, e_ref, o_ref, buf_ref, state_ref, wdtb_ref, band_ref):
    Q, N, G = CHUNK, SSM_STATE, SSM_GROUPS
    R = SSM_HEADS // G
    GW = D_BRANCH // G

    act = _silu(_causal_conv(buf_ref, [xs_ref[...], bc_ref[...]], cw_ref, cb_ref, band_ref))
    xs = act[:, :D_BRANCH]
    bm = act[:, D_BRANCH:D_BRANCH + G * N]
    cm = act[:, D_BRANCH + G * N:]

    head_lane = lax.broadcasted_iota(jnp.int32, (Q, LANES), 1) < SSM_HEADS
    dt_raw = lax.dot_general(h_ref[...], wdtb_ref[...], NT_DIMS, preferred_element_type=F32)
    dt = jnp.where(head_lane, _softplus(dt_raw + dtb_ref[...]), 0.0)
    da = dt * (-jnp.exp(alog_ref[...]))

    row = lax.broadcasted_iota(jnp.int32, (Q, Q), 0)
    col = lax.broadcasted_iota(jnp.int32, (Q, Q), 1)
    tri = row >= col
    cs3 = jnp.dot(band_ref[CONV_WIDTH - 1], jnp.concatenate([p.astype(BF16) for p in _split_bf16(da)], axis=1),
                  preferred_element_type=F32)
    da_cs = cs3[:, :LANES] + cs3[:, LANES:2 * LANES] + cs3[:, 2 * LANES:]
    cs2 = da_cs * LOG2E
    cs2T = cs2.T

    def expand(v):
        lhs = jnp.concatenate([p.astype(BF16) for p in _split_bf16(v)], axis=1)
        return jnp.dot(lhs, e_ref[...], preferred_element_type=F32)

    dt_exp = expand(dt)
    cs_exp = expand(cs2)
    last_exp = cs_exp[Q - 1:Q, :]
    xdt = xs * dt_exp
    x_w = xdt * jnp.exp2(last_exp - cs_exp)
    chunk_decay = jnp.exp2(last_exp)
    in_decay = jnp.exp2(cs_exp)

    lane = lax.broadcasted_iota(jnp.int32, (Q, GW), 1)
    for g in range(G):
        gs = slice(g * GW, (g + 1) * GW)
        bg = bm[:, g * N:(g + 1) * N]
        cgb = cm[:, g * N:(g + 1) * N].astype(BF16)
        bgT = bg.T.astype(BF16)
        cb = jnp.dot(cgb, bgT, preferred_element_type=F32)
        st = state_ref[:, gs]
        y_off = jnp.dot(cgb, st.astype(BF16), preferred_element_type=F32)
        state_ref[:, gs] = st * chunk_decay[:, gs] + jnp.dot(
            bgT, x_w[:, gs].astype(BF16), preferred_element_type=F32)
        xdt_g = xdt[:, gs]
        w_parts, x_parts = [], []
        for r in range(R):
            hd = g * R + r
            seg = cs2[:, hd:hd + 1] - cs2T[hd:hd + 1, :]
            lmat = jnp.exp2(jnp.where(tri, seg, -jnp.inf))
            w_parts.append((cb * lmat).astype(BF16))
            in_head = jnp.logical_and(lane >= r * SSM_HEAD_DIM, lane < (r + 1) * SSM_HEAD_DIM)
            x_parts.append(jnp.where(in_head, xdt_g, 0.0).astype(BF16))
        y_diag = jnp.dot(jnp.concatenate(w_parts, axis=1), jnp.concatenate(x_parts, axis=0),
                         preferred_element_type=F32)
        y = y_diag + y_off * in_decay[:, gs] + dexp_ref[:, gs] * xs[:, gs]
        y = y * _silu(z_ref[:, gs].astype(F32))
        ms = jnp.mean(y * y, axis=-1, keepdims=True)
        o_ref[:, gs] = (y * lax.rsqrt(ms + NORM_EPS) * nw_ref[:, gs]).astype(o_ref.dtype)


def _lru_chunk(x_ref, g_ref, cw_ref, cb_ref, wa_ref, ba_ref, wx_ref, bx_ref, lam_ref,
               o_ref, buf_ref, carry_ref, band_ref):
    n = x_ref.shape[0]
    xc = _causal_conv(buf_ref, [x_ref[...]], cw_ref, cb_ref, band_ref)
    ra, ix = [], []
    for blk in range(LRU_BLOCKS):
        xb = xc[:, blk * LRU_BLOCK_DIM:(blk + 1) * LRU_BLOCK_DIM].astype(BF16)
        ra.append(jnp.dot(xb, wa_ref[blk], preferred_element_type=F32))
        ix.append(jnp.dot(xb, wx_ref[blk], preferred_element_type=F32))
    r = _sigmoid(jnp.concatenate(ra, axis=1) + ba_ref[...])
    i = _sigmoid(jnp.concatenate(ix, axis=1) + bx_ref[...])
    log_a = -LRU_C * r * _softplus(-lam_ref[...])
    a = jnp.exp(log_a)
    u = jnp.sqrt(-jnp.tanh(log_a) * (1.0 + a * a)) * (i * xc)

    width = a.shape[1]
    a = a.reshape(n // SUBLANES, SUBLANES, width)
    u = u.reshape(n // SUBLANES, SUBLANES, width)
    sub = lax.broadcasted_iota(jnp.int32, (1, SUBLANES, width), 1)
    d = 1
    while d < SUBLANES:
        keep = sub >= d
        a_sh = jnp.where(keep, pltpu.roll(a, d, 1), 1.0)
        u_sh = jnp.where(keep, pltpu.roll(u, d, 1), 0.0)
        u = a * u_sh + u
        a = a * a_sh
        d *= 2
    h = carry_ref[...]
    groups = []
    for r in range(n // SUBLANES):
        groups.append(u[r] + a[r] * h)
        h = groups[-1][SUBLANES - 1:SUBLANES, :]
    carry_ref[...] = h
    hseq = jnp.concatenate(groups, axis=0)
    o_ref[...] = (hseq * _silu(g_ref[...].astype(F32))).astype(o_ref.dtype)


N_RET_IN, N_SSD_IN, N_LRU_IN = 5, 12, 9


def _mixers_kernel(*refs):
    ret_in = refs[:N_RET_IN]
    ssd_in = refs[N_RET_IN:N_RET_IN + N_SSD_IN]
    lru_in = refs[N_RET_IN + N_SSD_IN:N_RET_IN + N_SSD_IN + N_LRU_IN]
    (ret_out, ssd_out, lru_out, ret_state, ssd_buf, ssd_state, wdtb_ref, lru_buf,
     lru_carry, decay_ref, band_ref) = refs[N_RET_IN + N_SSD_IN + N_LRU_IN:]
    z_ref, xs_ref, bc_ref, h_ref, wdt_ref = ssd_in[:5]

    @pl.when(pl.program_id(1) == 0)
    def _():
        ret_state[...] = jnp.zeros_like(ret_state)
        ssd_state[...] = jnp.zeros_like(ssd_state)
        lru_carry[...] = jnp.zeros_like(lru_carry)
        ssd_buf[0:SUBLANES, :] = jnp.zeros((SUBLANES, ssd_buf.shape[1]), F32)
        lru_buf[0:SUBLANES, :] = jnp.zeros((SUBLANES, lru_buf.shape[1]), F32)
        wdtb_ref[...] = wdt_ref[...].astype(BF16)
        _retention_decays(decay_ref)
        _band_matrices(band_ref)

    _retention_chunk(*ret_in, ret_out, ret_state, decay_ref)
    _ssd_chunk(z_ref, xs_ref, bc_ref, h_ref, *ssd_in[5:], ssd_out, ssd_buf, ssd_state, wdtb_ref,
               band_ref)
    _lru_chunk(*lru_in, lru_out, lru_buf, lru_carry, band_ref)


def _mixers(proj, proj_lru, hnorm, w_in_t, layer, ret_gn_w, ssm_conv_w, ssm_conv_b, ssm_dt_bias,
            ssm_a_log, ssm_d, ssm_norm_w, lru_conv_w, lru_conv_b, lru_w_a, lru_b_a, lru_w_x,
            lru_b_x, lru_lambda, batch, seq):
    nc = seq // CHUNK
    t = batch * seq
    d = hnorm.shape[1]
    H = SSM_HEADS
    assert COL_DT % LANES == 0 and H <= LANES
    pad = lambda v: jnp.pad(v.reshape(1, H), ((0, 0), (0, LANES - H)))
    expand = np.zeros((LANES, D_BRANCH), np.float32)
    for hd in range(H):
        expand[hd, hd * SSM_HEAD_DIM:(hd + 1) * SSM_HEAD_DIM] = 1.0
    d_exp = jnp.repeat(ssm_d, SSM_HEAD_DIM).reshape(1, D_BRANCH)
    const = lambda shape: pl.BlockSpec(shape, lambda b, c: (0,) * len(shape))
    blk = lambda col0: pl.BlockSpec((CHUNK, D_BRANCH), lambda b, c: (b * nc + c, col0 // D_BRANCH))
    wshape = (LRU_BLOCKS, LRU_BLOCK_DIM, LRU_BLOCK_DIM)
    ret_specs = [blk(COL_RET + j * D_BRANCH) for j in range(4)] + [
        pl.BlockSpec((RET_HEADS, 1, RET_HEAD_DIM), lambda b, c: (0, 0, 0))]
    ssd_specs = [blk(COL_Z), blk(COL_XS), blk(COL_BC),
                 pl.BlockSpec((CHUNK, d), lambda b, c: (b * nc + c, 0)),
                 pl.BlockSpec((None, LANES, d), lambda b, c: (layer, COL_DT // LANES, 0)),
                 const((CONV_WIDTH, SSM_CONV_DIM)), const((1, SSM_CONV_DIM)),
                 const((1, LANES)), const((1, LANES)),
                 const((1, D_BRANCH)), const((1, D_BRANCH)),
                 const((BF16_TERMS * LANES, D_BRANCH))]
    lru_specs = [blk(0), blk(D_BRANCH),
                 const((CONV_WIDTH, D_BRANCH)), const((1, D_BRANCH)),
                 const(wshape), const((1, D_BRANCH)), const(wshape), const((1, D_BRANCH)),
                 const((1, D_BRANCH))]
    assert (len(ret_specs), len(ssd_specs), len(lru_specs)) == (N_RET_IN, N_SSD_IN, N_LRU_IN)
    out_spec = pl.BlockSpec((CHUNK, D_BRANCH), lambda b, c: (b * nc + c, 0))
    out_shape = jax.ShapeDtypeStruct((t, D_BRANCH), BF16)
    return pl.pallas_call(
        _mixers_kernel,
        grid=(batch, nc),
        in_specs=ret_specs + ssd_specs + lru_specs,
        out_specs=[out_spec] * 3,
        out_shape=[out_shape] * 3,
        scratch_shapes=[pltpu.VMEM((RET_HEADS, RET_HEAD_DIM, RET_HEAD_DIM), F32),
                        pltpu.VMEM((2 * SUBLANES, SSM_CONV_DIM), F32),
                        pltpu.VMEM((SSM_STATE, D_BRANCH), F32),
                        pltpu.VMEM((LANES, d), BF16),
                        pltpu.VMEM((2 * SUBLANES, D_BRANCH), F32),
                        pltpu.VMEM((1, D_BRANCH), F32),
                        pltpu.VMEM((RET_HEADS, 3, CHUNK, CHUNK), F32),
                        pltpu.VMEM((CONV_WIDTH, CHUNK, CHUNK), BF16)],
        compiler_params=_params("parallel", "arbitrary"),
        name="mixers",
    )(proj, proj, proj, proj, ret_gn_w.reshape(RET_HEADS, 1, RET_HEAD_DIM),
      proj, proj, proj, hnorm, w_in_t, ssm_conv_w, ssm_conv_b.reshape(1, -1),
      pad(ssm_dt_bias), pad(ssm_a_log), d_exp, ssm_norm_w.reshape(1, D_BRANCH),
      jnp.asarray(np.tile(expand, (BF16_TERMS, 1)), BF16),
      proj_lru, proj_lru, lru_conv_w, lru_conv_b.reshape(1, -1), lru_w_a.astype(BF16),
      lru_b_a.reshape(1, -1), lru_w_x.astype(BF16), lru_b_x.reshape(1, -1),
      lru_lambda.reshape(1, -1))


def _outproj_kernel(y0_ref, y1_ref, y2_ref, y3_ref, w_ref, x_ref, o_ref, wb_ref):
    @pl.when(pl.program_id(1) == 0)
    def _():
        wb_ref[...] = w_ref[...].astype(BF16)

    y = jnp.concatenate([y0_ref[...], y1_ref[...], y2_ref[...], y3_ref[...]], axis=1)
    o_ref[...] = x_ref[...] + jnp.dot(y, wb_ref[...], preferred_element_type=F32)


def _outproj(ys, w, layer, x):
    t, d = x.shape
    k = w.shape[1]
    tm = min(512, t)
    tn = 1024
    yspec = pl.BlockSpec((tm, D_BRANCH), lambda j, i: (i, 0))
    return pl.pallas_call(
        _outproj_kernel,
        grid=(d // tn, t // tm),
        in_specs=[yspec, yspec, yspec, yspec,
                  pl.BlockSpec((None, k, tn), lambda j, i: (layer, 0, j)),
                  pl.BlockSpec((tm, tn), lambda j, i: (i, j))],
        out_specs=pl.BlockSpec((tm, tn), lambda j, i: (i, j)),
        out_shape=jax.ShapeDtypeStruct((t, d), F32),
        scratch_shapes=[pltpu.VMEM((k, tn), BF16)],
        compiler_params=_params("parallel", "arbitrary"),
        name="outproj",
    )(*ys, w, x)


def kernel(x, norm_w, w_in, ret_gn_w, ssm_conv_w, ssm_conv_b, ssm_dt_bias, ssm_a_log, ssm_d, ssm_norm_w, lru_conv_w, lru_conv_b, lru_w_a, lru_b_a, lru_w_x, lru_b_x, lru_lambda, w_out, final_norm_w):
    batch, seq, d = x.shape
    depth = w_in.shape[0]
    assert seq % CHUNK == 0 and d == 4 * D_BRANCH
    t = batch * seq
    xf = x.reshape(t, d)
    w_in_t = jnp.swapaxes(w_in, 1, 2)
    for l in range(depth):
        hn = _rmsnorm(xf, norm_w[l], BF16)
        proj = _inproj(hn, w_in_t, l, 0, D_MAIN)
        proj_lru = _inproj(hn, w_in_t, l, COL_LRU, 2 * D_BRANCH)
        y_moba = _moba(proj, batch, seq)
        y_ret, y_ssm, y_lru = _mixers(
            proj, proj_lru, hn, w_in_t, l, ret_gn_w[l], ssm_conv_w[l], ssm_conv_b[l],
            ssm_dt_bias[l], ssm_a_log[l], ssm_d[l], ssm_norm_w[l], lru_conv_w[l], lru_conv_b[l],
            lru_w_a[l], lru_b_a[l], lru_w_x[l], lru_b_x[l], lru_lambda[l], batch, seq)
        xf = _outproj((y_ret, y_moba, y_ssm, y_lru), w_out, l, xf)
    return _rmsnorm(xf, final_norm_w, F32).reshape(batch, seq, d)
```

```python
import functools
import math

import jax
import jax.numpy as jnp
import numpy as np
from jax import lax
from jax.experimental import pallas as pl
from jax.experimental.pallas import tpu as pltpu

F32 = jnp.float32
BF16 = jnp.bfloat16

NORM_EPS = 1e-6
LOG2E = math.log2(math.e)
D_BRANCH = 1024
CHUNK = 256
RET_HEADS = 4
RET_HEAD_DIM = 256
MOBA_HEADS = 8
MOBA_HEAD_DIM = 128
MOBA_TOPK = 3
MOBA_HEADS_PER_STEP = 8
SSM_HEADS = 16
SSM_HEAD_DIM = 64
SSM_GROUPS = 4
SSM_STATE = 128
SSM_CONV_DIM = 2048
CONV_WIDTH = 4
LRU_BLOCKS = 8
LRU_BLOCK_DIM = 128
LRU_C = 8.0
LANES = 128
SUBLANES = 8
VMEM_LIMIT = 61 * 1024 * 1024

COL_RET = 0
COL_MOBA = 4 * D_BRANCH
COL_Z = 8 * D_BRANCH
COL_XS = COL_Z + D_BRANCH
COL_BC = COL_XS + D_BRANCH
COL_DT = COL_XS + SSM_CONV_DIM
COL_LRU = COL_DT + SSM_HEADS
D_MAIN = COL_DT

NT_DIMS = (((1,), (1,)), ((), ()))


def _params(*sem):
    return pltpu.CompilerParams(dimension_semantics=sem, vmem_limit_bytes=VMEM_LIMIT)


def _sigmoid(x):
    return 1.0 / (1.0 + jnp.exp(-x))


def _silu(x):
    return x * _sigmoid(x)


def _softplus(x):
    return jnp.maximum(x, 0.0) + jnp.log1p(jnp.exp(-jnp.abs(x)))


def _rmsnorm_kernel(x_ref, w_ref, o_ref):
    x = x_ref[...]
    ms = jnp.mean(x * x, axis=-1, keepdims=True)
    o_ref[...] = (x * lax.rsqrt(ms + NORM_EPS) * w_ref[...]).astype(o_ref.dtype)


def _rmsnorm(x, w, out_dtype):
    t, d = x.shape
    tm = min(512, t)
    return pl.pallas_call(
        _rmsnorm_kernel,
        grid=(t // tm,),
        in_specs=[pl.BlockSpec((tm, d), lambda i: (i, 0)),
                  pl.BlockSpec((1, d), lambda i: (0, 0))],
        out_specs=pl.BlockSpec((tm, d), lambda i: (i, 0)),
        out_shape=jax.ShapeDtypeStruct((t, d), out_dtype),
        compiler_params=_params("parallel"),
        name="rmsnorm",
    )(x, w.reshape(1, d))


def _inproj_kernel(a_ref, wt_ref, o_ref, wb_ref):
    @pl.when(pl.program_id(1) == 0)
    def _():
        wb_ref[...] = wt_ref[...].astype(BF16)

    o_ref[...] = lax.dot_general(a_ref[...], wb_ref[...], NT_DIMS,
                                 preferred_element_type=F32).astype(o_ref.dtype)


def _inproj(h, wt, layer, row0, n):
    t, k = h.shape
    tm = min(1024, t)
    tn = 1024
    assert n % tn == 0 and row0 % SUBLANES == 0
    once = pl.Buffered(1)
    if row0 % tn == 0:
        wspec = pl.BlockSpec((None, tn, k), lambda j, i: (layer, row0 // tn + j, 0),
                             pipeline_mode=once)
    else:
        wspec = pl.BlockSpec((None, pl.Element(tn), pl.Element(k)),
                             lambda j, i: (layer, pl.multiple_of(row0 + tn * j, SUBLANES), 0),
                             pipeline_mode=once)
    return pl.pallas_call(
        _inproj_kernel,
        grid=(n // tn, t // tm),
        in_specs=[pl.BlockSpec((tm, k), lambda j, i: (i, 0)), wspec],
        out_specs=pl.BlockSpec((tm, tn), lambda j, i: (i, j)),
        out_shape=jax.ShapeDtypeStruct((t, n), BF16),
        scratch_shapes=[pltpu.VMEM((tn, k), BF16)],
        compiler_params=_params("parallel", "arbitrary"),
        name="inproj",
    )(h, wt)


def _retention_log_decay(h):
    return math.log1p(-(2.0 ** (-5.0 - h)))


def _retention_decays(decay_ref):
    C = CHUNK
    row = lax.broadcasted_iota(jnp.int32, (C, C), 0)
    col = lax.broadcasted_iota(jnp.int32, (C, C), 1)
    diff = (row - col).astype(F32)
    rowf = row.astype(F32)
    causal = row >= col
    for h in range(RET_HEADS):
        lg2 = _retention_log_decay(h) * LOG2E
        decay_ref[h, 0] = jnp.where(causal, jnp.exp2(lg2 * jnp.where(causal, diff, 0.0)), 0.0)
        decay_ref[h, 1] = jnp.exp2(lg2 * (rowf + 1.0))
        decay_ref[h, 2] = jnp.exp2(lg2 * (C - 1.0 - rowf))


def _retention_chunk(q_ref, k_ref, v_ref, g_ref, gn_ref, o_ref, state_ref, decay_ref):
    C, dh = CHUNK, RET_HEAD_DIM
    for h in range(RET_HEADS):
        log_g = _retention_log_decay(h)
        sl = slice(h * dh, (h + 1) * dh)
        dmat, q_decay, k_decay = decay_ref[h, 0], decay_ref[h, 1], decay_ref[h, 2]
        qb = q_ref[:, sl]
        k = k_ref[:, sl].astype(F32) * (dh ** -0.5)
        kb = k.astype(BF16)
        vb = v_ref[:, sl]
        state = state_ref[h]
        inner = lax.dot_general(qb, kb, NT_DIMS, preferred_element_type=F32) * dmat
        y = jnp.dot(inner.astype(BF16), vb, preferred_element_type=F32)
        y = y + jnp.dot(qb, state.astype(BF16), preferred_element_type=F32) * q_decay
        kdT = (k * k_decay).T.astype(BF16)
        state_ref[h] = math.exp(log_g * C) * state + jnp.dot(kdT, vb, preferred_element_type=F32)
        ms = jnp.mean(y * y, axis=-1, keepdims=True)
        y = y * lax.rsqrt(ms + NORM_EPS) * gn_ref[h]
        o_ref[:, sl] = (y * _silu(g_ref[:, sl].astype(F32))).astype(o_ref.dtype)


BF16_TERMS = 3
MOBA_VT_ROWS = MOBA_HEAD_DIM + 16
MOBA_M_INIT = -1e30


def _split_bf16(x):
    terms = []
    for _ in range(BF16_TERMS - 1):
        hi = x.astype(BF16).astype(F32)
        terms.append(hi)
        x = x - hi
    return terms + [x]


def _moba_kernel(slopes_ref, q_ref, k_ref, v_ref, g_ref, o_ref, kmean_ref, kx_ref, vt_ref, sel_ref,
                 s_ref, acc_ref, m_ref):
    hg = pl.program_id(1)
    i = pl.program_id(2)
    L, dh, HP = CHUNK, MOBA_HEAD_DIM, MOBA_HEADS_PER_STEP
    seq = k_ref.shape[0]
    nb = seq // L
    nbp = kmean_ref.shape[1]
    neg = -jnp.inf
    lane = lax.broadcasted_iota(jnp.int32, (L, LANES), 1)

    @pl.when(i == 0)
    def _():
        kmean_ref[...] = jnp.zeros_like(kmean_ref)
        vt_ref[:, :, dh:, :] = jnp.ones((HP, nb, MOBA_VT_ROWS - dh, L), BF16)
        within = lax.broadcasted_iota(jnp.int32, (L, LANES), 0).astype(F32)
        for n in range(nb):
            kx = jnp.where(lane < BF16_TERMS, within,
                           jnp.where(lane < 2 * BF16_TERMS, float(n), 0.0))
            kx_ref[n * L:(n + 1) * L, :] = kx.astype(BF16)
            for hh in range(HP):
                hs = slice(hh * dh, (hh + 1) * dh)
                kn = k_ref[n * L:(n + 1) * L, hs].astype(F32)
                kmean_ref[hh, n:n + 1, :] = jnp.mean(kn, axis=0, keepdims=True)
                vn = v_ref[n * L:(n + 1) * L, hs].astype(F32)
                vt_ref[hh, n, 0:dh, :] = vn.T.astype(BF16)

    blk_row = lax.broadcasted_iota(jnp.int32, (nbp, L), 0)
    blk_rowf = blk_row.astype(F32)
    key = lax.broadcasted_iota(jnp.int32, (L, L), 0)
    qry = lax.broadcasted_iota(jnp.int32, (L, L), 1)
    lane1 = lax.broadcasted_iota(jnp.int32, (1, LANES), 1)

    def scores(hh, qa, off):
        hs = slice(hh * dh, (hh + 1) * dh)
        ka = jnp.concatenate([k_ref[pl.ds(off, L), hs], kx_ref[pl.ds(off, L), :]], axis=1)
        return lax.dot_general(ka, qa, NT_DIMS, preferred_element_type=F32)

    qas = []
    for hh in range(HP):
        hs = slice(hh * dh, (hh + 1) * dh)
        qb = q_ref[:, hs]
        q = qb.astype(F32)
        km3 = jnp.concatenate([p.astype(BF16) for p in _split_bf16(kmean_ref[hh])], axis=0)
        gate3 = lax.dot_general(km3, qb, NT_DIMS, preferred_element_type=F32)
        gate = gate3[:nbp] + gate3[nbp:2 * nbp] + gate3[2 * nbp:]
        gm = jnp.where(blk_row < i, gate, neg)
        sel = jnp.zeros(gate.shape, F32)
        for _ in range(MOBA_TOPK):
            mx = jnp.max(gm, axis=0, keepdims=True)
            first = jnp.min(jnp.where(gm == mx, blk_rowf, float(nbp)), axis=0, keepdims=True)
            pick = jnp.logical_and(blk_rowf == first, mx > neg)
            sel = jnp.where(pick, 1.0, sel)
            gm = jnp.where(pick, neg, gm)
        sel_ref[hh] = sel
        slope2 = jnp.full((1, LANES), slopes_ref[hg * HP + hh] * LOG2E, F32)
        qx = jnp.zeros((1, LANES), F32)
        for t, term in enumerate(_split_bf16(slope2) + _split_bf16(slope2 * L)):
            qx = jnp.where(lane1 == t, term, qx)
        qa = jnp.concatenate([(q * (dh ** -0.5 * LOG2E)).astype(BF16),
                              jnp.broadcast_to(qx, (L, LANES)).astype(BF16)], axis=1)
        qas.append(qa)
        m_ref[hh] = jnp.full((1, L), MOBA_M_INIT, F32)
        acc_ref[hh] = jnp.zeros((MOBA_VT_ROWS, L), F32)
        s_ref[0, hh] = scores(hh, qa, 0)
        s_ref[1, hh] = scores(hh, qa, pl.multiple_of(jnp.minimum(i, 1) * L, L))

    def update(hh, blk, s, picked):
        m = m_ref[hh]
        cmax = jnp.max(s, axis=0, keepdims=True)
        if picked is None:
            m_new = jnp.maximum(m, cmax)
            p = jnp.exp2(s - m_new)
        else:
            m_new = jnp.maximum(m, jnp.where(picked, cmax, neg))
            p = jnp.exp2(s - jnp.where(picked, m_new, jnp.inf))
        m_ref[hh] = m_new
        acc_ref[hh] = acc_ref[hh] * jnp.exp2(m - m_new) + jnp.dot(
            vt_ref[hh, blk], p.astype(BF16), preferred_element_type=F32)

    def past_block(hh, blk, slot):
        picked = sel_ref[hh, pl.ds(blk, 1), :] > 0.0
        update(hh, blk, s_ref[slot, hh], picked)

    @pl.loop(0, jnp.right_shift(i, 1))
    def _(pair):
        blk = 2 * pair
        nxt0 = pl.multiple_of((blk + 2) * L, L)
        nxt1 = pl.multiple_of(jnp.minimum(blk + 3, i) * L, L)
        for hh in range(HP):
            past_block(hh, blk, 0)
            s_ref[0, hh] = scores(hh, qas[hh], nxt0)
            past_block(hh, blk + 1, 1)
            s_ref[1, hh] = scores(hh, qas[hh], nxt1)

    odd = jnp.bitwise_and(i, 1) == 1

    @pl.when(odd)
    def _():
        for hh in range(HP):
            past_block(hh, i - 1, 0)

    for hh in range(HP):
        hs = slice(hh * dh, (hh + 1) * dh)
        s_own = jnp.where(odd, s_ref[1, hh], s_ref[0, hh])
        update(hh, i, jnp.where(key <= qry, s_own, neg), None)
        acc = acc_ref[hh]
        out = (acc[:dh, :] / acc[dh:dh + 1, :]).T
        o_ref[:, hs] = (out * _silu(g_ref[:, hs].astype(F32))).astype(o_ref.dtype)


def _moba(proj, batch, seq):
    nb = seq // CHUNK
    t = batch * seq
    H, dh, HP = MOBA_HEADS, MOBA_HEAD_DIM, MOBA_HEADS_PER_STEP
    nbp = -(-nb // SUBLANES) * SUBLANES
    w = HP * dh
    base = COL_MOBA // w
    per = D_BRANCH // w
    slopes = jnp.asarray(np.exp2(-8.0 * (np.arange(H, dtype=np.float64) + 1.0) / H), F32)
    once = pl.Buffered(1)
    return pl.pallas_call(
        _moba_kernel,
        grid=(batch, H // HP, nb),
        in_specs=[pl.BlockSpec(memory_space=pltpu.SMEM),
                  pl.BlockSpec((CHUNK, w), lambda b, h, i: (b * nb + i, base + h)),
                  pl.BlockSpec((seq, w), lambda b, h, i: (b, base + per + h), pipeline_mode=once),
                  pl.BlockSpec((seq, w), lambda b, h, i: (b, base + 2 * per + h), pipeline_mode=once),
                  pl.BlockSpec((CHUNK, w), lambda b, h, i: (b * nb + i, base + 3 * per + h))],
        out_specs=pl.BlockSpec((CHUNK, w), lambda b, h, i: (b * nb + i, h)),
        out_shape=jax.ShapeDtypeStruct((t, D_BRANCH), BF16),
        scratch_shapes=[pltpu.VMEM((HP, nbp, dh), F32),
                        pltpu.VMEM((seq, LANES), BF16),
                        pltpu.VMEM((HP, nb, MOBA_VT_ROWS, CHUNK), BF16),
                        pltpu.VMEM((HP, nbp, CHUNK), F32),
                        pltpu.VMEM((2, HP, CHUNK, CHUNK), F32),
                        pltpu.VMEM((HP, MOBA_VT_ROWS, CHUNK), F32),
                        pltpu.VMEM((HP, 1, CHUNK), F32)],
        compiler_params=_params("parallel", "parallel", "arbitrary"),
        name="moba",
    )(slopes, proj, proj, proj, proj)


def _band_matrices(band_ref):
    n = band_ref.shape[1]
    row = lax.broadcasted_iota(jnp.int32, (n, n), 0)
    col = lax.broadcasted_iota(jnp.int32, (n, n), 1)
    for k in range(CONV_WIDTH - 1):
        band_ref[k] = jnp.where(row - col == CONV_WIDTH - 1 - k, 1.0, 0.0).astype(BF16)
    band_ref[CONV_WIDTH - 1] = jnp.where(row >= col, 1.0, 0.0).astype(BF16)


def _causal_conv(buf_ref, parts, cw_ref, cb_ref, band_ref):
    n = parts[0].shape[0]
    x = jnp.concatenate(parts, axis=1)
    xf = x.astype(F32)
    buf_ref[SUBLANES:2 * SUBLANES, :] = xf[0:SUBLANES, :]
    y = cb_ref[...] + xf * cw_ref[CONV_WIDTH - 1:CONV_WIDTH, :]
    head = cb_ref[...] + xf[0:SUBLANES, :] * cw_ref[CONV_WIDTH - 1:CONV_WIDTH, :]
    for k in range(CONV_WIDTH - 1):
        back = CONV_WIDTH - 1 - k
        y = y + jnp.dot(band_ref[k], x, preferred_element_type=F32) * cw_ref[k:k + 1, :]
        head = head + buf_ref[SUBLANES - back:2 * SUBLANES - back, :] * cw_ref[k:k + 1, :]
    buf_ref[0:SUBLANES, :] = xf[n - SUBLANES:n, :]
    return jnp.concatenate([head, y[SUBLANES:, :]], axis=0)


def _ssd_chunk(z_ref, xs_ref, bc_ref, h_ref, cw_ref, cb_ref, dtb_ref, alog_ref,
               dexp_ref, nw_ref, e_ref, o_ref, buf_ref, state_ref, wdtb_ref, band_ref):
    Q, N, G = CHUNK, SSM_STATE, SSM_GROUPS
    R = SSM_HEADS // G
    GW = D_BRANCH // G

    act = _silu(_causal_conv(buf_ref, [xs_ref[...], bc_ref[...]], cw_ref, cb_ref, band_ref))
    xs = act[:, :D_BRANCH]
    bm = act[:, D_BRANCH:D_BRANCH + G * N]
    cm = act[:, D_BRANCH + G * N:]

    head_lane = lax.broadcasted_iota(jnp.int32, (Q, LANES), 1) < SSM_HEADS
    dt_raw = lax.dot_general(h_ref[...], wdtb_ref[...], NT_DIMS, preferred_element_type=F32)
    dt = jnp.where(head_lane, _softplus(dt_raw + dtb_ref[...]), 0.0)
    da = dt * (-jnp.exp(alog_ref[...]))

    row = lax.broadcasted_iota(jnp.int32, (Q, Q), 0)
    col = lax.broadcasted_iota(jnp.int32, (Q, Q), 1)
    tri = row >= col
    cs3 = jnp.dot(band_ref[CONV_WIDTH - 1], jnp.concatenate([p.astype(BF16) for p in _split_bf16(da)], axis=1),
                  preferred_element_type=F32)
    da_cs = cs3[:, :LANES] + cs3[:, LANES:2 * LANES] + cs3[:, 2 * LANES:]
    cs2 = da_cs * LOG2E
    cs2T = cs2.T

    def expand(v):
        lhs = jnp.concatenate([p.astype(BF16) for p in _split_bf16(v)], axis=1)
        return jnp.dot(lhs, e_ref[...], preferred_element_type=F32)

    dt_exp = expand(dt)
    cs_exp = expand(cs2)
    last_exp = cs_exp[Q - 1:Q, :]
    xdt = xs * dt_exp
    x_w = xdt * jnp.exp2(last_exp - cs_exp)
    chunk_decay = jnp.exp2(last_exp)
    in_decay = jnp.exp2(cs_exp)

    lane = lax.broadcasted_iota(jnp.int32, (Q, GW), 1)
    for g in range(G):
        gs = slice(g * GW, (g + 1) * GW)
        bg = bm[:, g * N:(g + 1) * N]
        cgb = cm[:, g * N:(g + 1) * N].astype(BF16)
        bgT = bg.T.astype(BF16)
        cb = jnp.dot(cgb, bgT, preferred_element_type=F32)
        st = state_ref[:, gs]
        y_off = jnp.dot(cgb, st.astype(BF16), preferred_element_type=F32)
        state_ref[:, gs] = st * chunk_decay[:, gs] + jnp.dot(
            bgT, x_w[:, gs].astype(BF16), preferred_element_type=F32)
        xdt_g = xdt[:, gs]
        w_parts, x_parts = [], []
        for r in range(R):
            hd = g * R + r
            seg = cs2[:, hd:hd + 1] - cs2T[hd:hd + 1, :]
            lmat = jnp.exp2(jnp.where(tri, seg, -jnp.inf))
            w_parts.append((cb * lmat).astype(BF16))
            in_head = jnp.logical_and(lane >= r * SSM_HEAD_DIM, lane < (r + 1) * SSM_HEAD_DIM)
            x_parts.append(jnp.where(in_head, xdt_g, 0.0).astype(BF16))
        y_diag = jnp.dot(jnp.concatenate(w_parts, axis=1), jnp.concatenate(x_parts, axis=0),
                         preferred_element_type=F32)
        y = y_diag + y_off * in_decay[:, gs] + dexp_ref[:, gs] * xs[:, gs]
        y = y * _silu(z_ref[:, gs].astype(F32))
        ms = jnp.mean(y * y, axis=-1, keepdims=True)
        o_ref[:, gs] = (y * lax.rsqrt(ms + NORM_EPS) * nw_ref[:, gs]).astype(o_ref.dtype)


def _lru_chunk(x_ref, g_ref, cw_ref, cb_ref, wa_ref, ba_ref, wx_ref, bx_ref, lam_ref,
               o_ref, buf_ref, carry_ref, band_ref):
    n = x_ref.shape[0]
    xc = _causal_conv(buf_ref, [x_ref[...]], cw_ref, cb_ref, band_ref)
    ra, ix = [], []
    for blk in range(LRU_BLOCKS):
        xb = xc[:, blk * LRU_BLOCK_DIM:(blk + 1) * LRU_BLOCK_DIM].astype(BF16)
        ra.append(jnp.dot(xb, wa_ref[blk], preferred_element_type=F32))
        ix.append(jnp.dot(xb, wx_ref[blk], preferred_element_type=F32))
    r = _sigmoid(jnp.concatenate(ra, axis=1) + ba_ref[...])
    i = _sigmoid(jnp.concatenate(ix, axis=1) + bx_ref[...])
    log_a = -LRU_C * r * _softplus(-lam_ref[...])
    a = jnp.exp(log_a)
    u = jnp.sqrt(-jnp.tanh(log_a) * (1.0 + a * a)) * (i * xc)

    width = a.shape[1]
    a = a.reshape(n // SUBLANES, SUBLANES, width)
    u = u.reshape(n // SUBLANES, SUBLANES, width)
    sub = lax.broadcasted_iota(jnp.int32, (1, SUBLANES, width), 1)
    d = 1
    while d < SUBLANES:
        keep = sub >= d
        a_sh = jnp.where(keep, pltpu.roll(a, d, 1), 1.0)
        u_sh = jnp.where(keep, pltpu.roll(u, d, 1), 0.0)
        u = a * u_sh + u
        a = a * a_sh
        d *= 2
    h = carry_ref[...]
    groups = []
    for r in range(n // SUBLANES):
        groups.append(u[r] + a[r] * h)
        h = groups[-1][SUBLANES - 1:SUBLANES, :]
    carry_ref[...] = h
    hseq = jnp.concatenate(groups, axis=0)
    o_ref[...] = (hseq * _silu(g_ref[...].astype(F32))).astype(o_ref.dtype)


N_RET_IN, N_SSD_IN, N_LRU_IN = 5, 12, 9


def _mixers_kernel(*refs):
    ret_in = refs[:N_RET_IN]
    ssd_in = refs[N_RET_IN:N_RET_IN + N_SSD_IN]
    lru_in = refs[N_RET_IN + N_SSD_IN:N_RET_IN + N_SSD_IN + N_LRU_IN]
    (ret_out, ssd_out, lru_out, ret_state, ssd_buf, ssd_state, wdtb_ref, lru_buf,
     lru_carry, decay_ref, band_ref) = refs[N_RET_IN + N_SSD_IN + N_LRU_IN:]
    z_ref, xs_ref, bc_ref, h_ref, wdt_ref = ssd_in[:5]

    @pl.when(pl.program_id(1) == 0)
    def _():
        ret_state[...] = jnp.zeros_like(ret_state)
        ssd_state[...] = jnp.zeros_like(ssd_state)
        lru_carry[...] = jnp.zeros_like(lru_carry)
        ssd_buf[0:SUBLANES, :] = jnp.zeros((SUBLANES, ssd_buf.shape[1]), F32)
        lru_buf[0:SUBLANES, :] = jnp.zeros((SUBLANES, lru_buf.shape[1]), F32)
        wdtb_ref[...] = wdt_ref[...].astype(BF16)
        _retention_decays(decay_ref)
        _band_matrices(band_ref)

    _retention_chunk(*ret_in, ret_out, ret_state, decay_ref)
    _ssd_chunk(z_ref, xs_ref, bc_ref, h_ref, *ssd_in[5:], ssd_out, ssd_buf, ssd_state, wdtb_ref,
               band_ref)
    _lru_chunk(*lru_in, lru_out, lru_buf, lru_carry, band_ref)


def _mixers(proj, proj_lru, hnorm, w_in_t, layer, ret_gn_w, ssm_conv_w, ssm_conv_b, ssm_dt_bias,
            ssm_a_log, ssm_d, ssm_norm_w, lru_conv_w, lru_conv_b, lru_w_a, lru_b_a, lru_w_x,
            lru_b_x, lru_lambda, batch, seq):
    nc = seq // CHUNK
    t = batch * seq
    d = hnorm.shape[1]
    H = SSM_HEADS
    assert COL_DT % LANES == 0 and H <= LANES
    pad = lambda v: jnp.pad(v.reshape(1, H), ((0, 0), (0, LANES - H)))
    expand = np.zeros((LANES, D_BRANCH), np.float32)
    for hd in range(H):
        expand[hd, hd * SSM_HEAD_DIM:(hd + 1) * SSM_HEAD_DIM] = 1.0
    d_exp = jnp.repeat(ssm_d, SSM_HEAD_DIM).reshape(1, D_BRANCH)
    const = lambda shape: pl.BlockSpec(shape, lambda b, c: (0,) * len(shape))
    blk = lambda col0: pl.BlockSpec((CHUNK, D_BRANCH), lambda b, c: (b * nc + c, col0 // D_BRANCH))
    wshape = (LRU_BLOCKS, LRU_BLOCK_DIM, LRU_BLOCK_DIM)
    ret_specs = [blk(COL_RET + j * D_BRANCH) for j in range(4)] + [
        pl.BlockSpec((RET_HEADS, 1, RET_HEAD_DIM), lambda b, c: (0, 0, 0))]
    ssd_specs = [blk(COL_Z), blk(COL_XS), blk(COL_BC),
                 pl.BlockSpec((CHUNK, d), lambda b, c: (b * nc + c, 0)),
                 pl.BlockSpec((None, LANES, d), lambda b, c: (layer, COL_DT // LANES, 0)),
                 const((CONV_WIDTH, SSM_CONV_DIM)), const((1, SSM_CONV_DIM)),
                 const((1, LANES)), const((1, LANES)),
                 const((1, D_BRANCH)), const((1, D_BRANCH)),
                 const((BF16_TERMS * LANES, D_BRANCH))]
    lru_specs = [blk(0), blk(D_BRANCH),
                 const((CONV_WIDTH, D_BRANCH)), const((1, D_BRANCH)),
                 const(wshape), const((1, D_BRANCH)), const(wshape), const((1, D_BRANCH)),
                 const((1, D_BRANCH))]
    assert (len(ret_specs), len(ssd_specs), len(lru_specs)) == (N_RET_IN, N_SSD_IN, N_LRU_IN)
    out_spec = pl.BlockSpec((CHUNK, D_BRANCH), lambda b, c: (b * nc + c, 0))
    out_shape = jax.ShapeDtypeStruct((t, D_BRANCH), BF16)
    return pl.pallas_call(
        _mixers_kernel,
        grid=(batch, nc),
        in_specs=ret_specs + ssd_specs + lru_specs,
        out_specs=[out_spec] * 3,
        out_shape=[out_shape] * 3,
        scratch_shapes=[pltpu.VMEM((RET_HEADS, RET_HEAD_DIM, RET_HEAD_DIM), F32),
                        pltpu.VMEM((2 * SUBLANES, SSM_CONV_DIM), F32),
                        pltpu.VMEM((SSM_STATE, D_BRANCH), F32),
                        pltpu.VMEM((LANES, d), BF16),
                        pltpu.VMEM((2 * SUBLANES, D_BRANCH), F32),
                        pltpu.VMEM((1, D_BRANCH), F32),
                        pltpu.VMEM((RET_HEADS, 3, CHUNK, CHUNK), F32),
                        pltpu.VMEM((CONV_WIDTH, CHUNK, CHUNK), BF16)],
        compiler_params=_params("parallel", "arbitrary"),
        name="mixers",
    )(proj, proj, proj, proj, ret_gn_w.reshape(RET_HEADS, 1, RET_HEAD_DIM),
      proj, proj, proj, hnorm, w_in_t, ssm_conv_w, ssm_conv_b.reshape(1, -1),
      pad(ssm_dt_bias), pad(ssm_a_log), d_exp, ssm_norm_w.reshape(1, D_BRANCH),
      jnp.asarray(np.tile(expand, (BF16_TERMS, 1)), BF16),
      proj_lru, proj_lru, lru_conv_w, lru_conv_b.reshape(1, -1), lru_w_a.astype(BF16),
      lru_b_a.reshape(1, -1), lru_w_x.astype(BF16), lru_b_x.reshape(1, -1),
      lru_lambda.reshape(1, -1))


def _outproj_kernel(y0_ref, y1_ref, y2_ref, y3_ref, w_ref, x_ref, o_ref, wb_ref):
    @pl.when(pl.program_id(1) == 0)
    def _():
        wb_ref[...] = w_ref[...].astype(BF16)

    y = jnp.concatenate([y0_ref[...], y1_ref[...], y2_ref[...], y3_ref[...]], axis=1)
    o_ref[...] = x_ref[...] + jnp.dot(y, wb_ref[...], preferred_element_type=F32)


def _outproj(ys, w, layer, x):
    t, d = x.shape
    k = w.shape[1]
    tm = min(512, t)
    tn = 1024
    yspec = pl.BlockSpec((tm, D_BRANCH), lambda j, i: (i, 0))
    return pl.pallas_call(
        _outproj_kernel,
        grid=(d // tn, t // tm),
        in_specs=[yspec, yspec, yspec, yspec,
                  pl.BlockSpec((None, k, tn), lambda j, i: (layer, 0, j)),
                  pl.BlockSpec((tm, tn), lambda j, i: (i, j))],
        out_specs=pl.BlockSpec((tm, tn), lambda j, i: (i, j)),
        out_shape=jax.ShapeDtypeStruct((t, d), F32),
        scratch_shapes=[pltpu.VMEM((k, tn), BF16)],
        compiler_params=_params("parallel", "arbitrary"),
        name="outproj",
    )(*ys, w, x)


def kernel(x, norm_w, w_in, ret_gn_w, ssm_conv_w, ssm_conv_b, ssm_dt_bias, ssm_a_log, ssm_d, ssm_norm_w, lru_conv_w, lru_conv_b, lru_w_a, lru_b_a, lru_w_x, lru_b_x, lru_lambda, w_out, final_norm_w):
    batch, seq, d = x.shape
    depth = w_in.shape[0]
    assert seq % CHUNK == 0 and d == 4 * D_BRANCH
    t = batch * seq
    xf = x.reshape(t, d)
    w_in_t = jnp.swapaxes(w_in, 1, 2)
    for l in range(depth):
        hn = _rmsnorm(xf, norm_w[l], BF16)
        proj = _inproj(hn, w_in_t, l, 0, D_MAIN)
        proj_lru = _inproj(hn, w_in_t, l, COL_LRU, 2 * D_BRANCH)
        y_moba = _moba(proj, batch, seq)
        y_ret, y_ssm, y_lru = _mixers(
            proj, proj_lru, hn, w_in_t, l, ret_gn_w[l], ssm_conv_w[l], ssm_conv_b[l],
            ssm_dt_bias[l], ssm_a_log[l], ssm_d[l], ssm_norm_w[l], lru_conv_w[l], lru_conv_b[l],
            lru_w_a[l], lru_b_a[l], lru_w_x[l], lru_b_x[l], lru_lambda[l], batch, seq)
        xf = _outproj((y_ret, y_moba, y_ssm, y_lru), w_out, l, xf)
    return _rmsnorm(xf, final_norm_w, F32).reshape(batch, seq, d)
```
